```python
import math
import jax, jax.numpy as jnp
from jax import lax
import numpy as np

D_MODEL = 4096
BATCH = 2
SEQ = 8192
DEPTH = 2
DEC_BATCH = 16
DEC_SEQ = 32
PAST_LEN = 2048

CHUNK = 64
Q_BLOCK = 128
HEAD_DIM = 128
A_VAL = 3 * D_MODEL // 8
A_HEADS = A_VAL // HEAD_DIM
A_DK = HEAD_DIM
A_DV = HEAD_DIM
A_KEY = A_HEADS * A_DK
A_CONV_DIM = 2 * A_KEY + A_VAL
CONV_W = 4
C_VAL = 3 * D_MODEL // 8
C_HEADS = 4
C_DV = C_VAL // C_HEADS
C_DK = C_DV // 2
C_KEY = C_HEADS * C_DK
C_RANK = 16
C_TAU = 16.0
B_WIDTH = D_MODEL - A_VAL - C_VAL
B_HEADS = B_WIDTH // HEAD_DIM
B_HD = HEAD_DIM
MIX_WIDTH = A_VAL + B_WIDTH + C_VAL
D_FF = -(-8 * D_MODEL // (3 * 256)) * 256
IN_SIZES = (A_CONV_DIM, A_VAL, A_HEADS, A_HEADS,
            B_WIDTH, B_WIDTH, B_WIDTH, B_HEADS,
            C_KEY, C_KEY, C_VAL, C_VAL, C_RANK)
IN_DIM = sum(IN_SIZES)
DEEPNORM_ALPHA = (2 * DEPTH) ** 0.25
DEEPNORM_BETA = (8 * DEPTH) ** -0.25
LN_EPS = 1e-5
RMS_EPS = 1e-6

kernel_name = 'hybrid_streaming_encoder_step'


def layer_norm(x, g, b):
    xf = x.astype(jnp.float32)
    mu = jnp.mean(xf, -1, keepdims=True)
    var = jnp.mean(jnp.square(xf - mu), -1, keepdims=True)
    return ((xf - mu) * lax.rsqrt(var + LN_EPS) * g + b).astype(x.dtype)


def rms_norm(x, g):
    return x * lax.rsqrt(jnp.mean(x * x, -1, keepdims=True) + RMS_EPS) * g


def l2_normalize(x):
    return x * lax.rsqrt(jnp.sum(x * x, -1, keepdims=True) + RMS_EPS)


def split_cols(h):
    out, start = [], 0
    for n in IN_SIZES:
        out.append(h[..., start:start + n])
        start += n
    return out


def to_chunks(a, L):
    n, t = a.shape[:2]
    return jnp.swapaxes(a.reshape((n, t // L, L) + a.shape[2:]), 0, 1)


def from_chunks(a):
    a = jnp.swapaxes(a, 0, 1)
    return a.reshape((a.shape[0], a.shape[1] * a.shape[2]) + a.shape[3:])


def run_chunked(step, s0, xs):
    t = xs[0].shape[1]
    if t <= CHUNK:
        return step(s0, *xs)
    s, out = lax.scan(lambda s, c: step(s, *c), s0, tuple(to_chunks(a, CHUNK) for a in xs))
    return s, from_chunks(out)


def causal_conv(u, buf, w):
    t = u.shape[1]
    full = jnp.concatenate([buf, u], axis=1)
    out = full[:, 0:t] * w[0]
    for i in range(1, CONV_W):
        out = out + full[:, i:i + t] * w[i]
    return out, full[:, full.shape[1] - (CONV_W - 1):]


def gdn_chunk(S, q, k, v, g, beta):
    q, k, v = (jnp.swapaxes(a, 1, 2) for a in (q, k, v))
    g = jnp.cumsum(jnp.swapaxes(g, 1, 2), axis=-1)
    beta = jnp.swapaxes(beta, 1, 2)
    L = q.shape[2]
    idx = jnp.arange(L)
    diff = g[..., :, None] - g[..., None, :]
    dec_strict = jnp.exp(jnp.where(idx[:, None] > idx[None, :], diff, -jnp.inf))
    dec_incl = jnp.exp(jnp.where(idx[:, None] >= idx[None, :], diff, -jnp.inf))
    eg = jnp.exp(g)[..., None]
    lhs = jnp.eye(L, dtype=q.dtype) + beta[..., None] * jnp.einsum('nhtd,nhsd->nhts', k, k) * dec_strict
    rhs = beta[..., None] * (v - eg * jnp.einsum('nhtd,nhdv->nhtv', k, S))
    u = lax.linalg.triangular_solve(lhs, rhs, left_side=True, lower=True, unit_diagonal=True)
    o = eg * jnp.einsum('nhtd,nhdv->nhtv', q, S) + jnp.einsum(
        'nhts,nhsv->nhtv', jnp.einsum('nhtd,nhsd->nhts', q, k) * dec_incl, u)
    k_end = k * jnp.exp(g[..., -1:] - g)[..., None]
    S_new = eg[..., -1:, :] * S + jnp.einsum('nhsd,nhsv->nhdv', k_end, u)
    return S_new, jnp.swapaxes(o, 1, 2)


def gla_chunk(S, q, k, v, lg):
    L = q.shape[1]
    b = jnp.cumsum(lg, axis=1)
    idx = jnp.arange(L)
    causal = (idx[:, None] >= idx[None, :])[None, :, :, None, None]
    dec = jnp.exp(jnp.where(causal, b[:, :, None] - b[:, None, :], -jnp.inf))
    att = jnp.einsum('nthd,nshd,ntshd->nhts', q, k, dec)
    o = jnp.einsum('nthd,nhdv->nthv', q * jnp.exp(b), S) + jnp.einsum('nhts,nshv->nthv', att, v)
    S_new = jnp.exp(b[:, -1])[..., None] * S + jnp.einsum('nshd,nshv->nhdv', k * jnp.exp(b[:, -1:] - b), v)
    return S_new, o


def fox_attend(q, cq, pq, k, v, ck, pk):
    s = jnp.einsum('nqhd,nkhd->nhqk', q, k) * (B_HD ** -0.5)
    s = s + (jnp.swapaxes(cq, 1, 2)[..., :, None] - jnp.swapaxes(ck, 1, 2)[..., None, :])
    s = jnp.where(pk[None, :] <= pq[:, None], s, -jnp.inf)
    return jnp.einsum('nhqk,nkhd->nqhd', jax.nn.softmax(s, axis=-1), v)


def hybrid_mixer(x, conv_buf, s_a, past_b, s_c, w):
    f32 = jnp.float32
    n, t, _ = x.shape
    (qkv_a, z_a, a_a, b_a, q_b, k_b, v_b, f_b,
     q_c, k_c, v_c, r_c, lr_c) = split_cols((x @ w['w_in']).astype(f32))
    qkv_a, conv_new = causal_conv(qkv_a, conv_buf.astype(f32), w['conv_w'].astype(f32))
    qkv_a = jax.nn.silu(qkv_a)
    qa = l2_normalize(qkv_a[..., :A_KEY].reshape(n, t, A_HEADS, A_DK)) * (A_DK ** -0.5)
    ka = l2_normalize(qkv_a[..., A_KEY:2 * A_KEY].reshape(n, t, A_HEADS, A_DK))
    va = qkv_a[..., 2 * A_KEY:].reshape(n, t, A_HEADS, A_DV)
    g = -jnp.exp(w['a_log'].astype(f32)) * jax.nn.softplus(a_a + w['dt_bias'])
    beta = jax.nn.sigmoid(b_a)
    s_a_new, oa = run_chunked(gdn_chunk, s_a.astype(f32), (qa, ka, va, g, beta))
    oa = rms_norm(oa, w['a_norm_g']) * jax.nn.silu(z_a.reshape(n, t, A_HEADS, A_DV))
    qb = q_b.reshape(n, t, B_HEADS, B_HD)
    kb = k_b.reshape(n, t, B_HEADS, B_HD)
    vb = v_b.reshape(n, t, B_HEADS, B_HD)
    lf = jax.nn.log_sigmoid(f_b + w['f_bias'])
    if past_b is None:
        c = jnp.cumsum(lf, axis=1)
        pos = jnp.arange(t)
        ob = lax.map(lambda blk: fox_attend(blk[0], blk[1], blk[2], kb, vb, c, pos),
                     (to_chunks(qb, Q_BLOCK), to_chunks(c, Q_BLOCK), pos.reshape(t // Q_BLOCK, Q_BLOCK)))
        ob = from_chunks(ob)
    else:
        k_past, v_past, lf_past = past_b
        p = k_past.shape[1]
        k_all = jnp.concatenate([k_past.astype(f32), kb], axis=1)
        v_all = jnp.concatenate([v_past.astype(f32), vb], axis=1)
        c_all = jnp.cumsum(jnp.concatenate([lf_past.astype(f32), lf], axis=1), axis=1)
        pos_all = jnp.arange(p + t)
        ob = fox_attend(qb, c_all[:, p:], pos_all[p:], k_all, v_all, c_all, pos_all)
    qc = q_c.reshape(n, t, C_HEADS, C_DK) * (C_DK ** -0.5)
    kc = k_c.reshape(n, t, C_HEADS, C_DK)
    vc = v_c.reshape(n, t, C_HEADS, C_DV)
    lg = (jax.nn.log_sigmoid(lr_c @ w['c_w2'] + w['c_b2']) / C_TAU).reshape(n, t, C_HEADS, C_DK)
    s_c_new, oc = run_chunked(gla_chunk, s_c.astype(f32), (qc, kc, vc, lg))
    oc = rms_norm(oc, w['c_norm_g']) * jax.nn.silu(r_c.reshape(n, t, C_HEADS, C_DV))
    o = jnp.concatenate([oa.reshape(n, t, A_VAL), ob.reshape(n, t, B_WIDTH), oc.reshape(n, t, C_VAL)],
                        axis=-1).astype(x.dtype)
    return o @ w['w_out'], (conv_new, s_a_new, kb, vb, lf, s_c_new)


def trunk_layer(x, conv_buf, s_a, past_b, s_c, w):
    y, st = hybrid_mixer(x, conv_buf, s_a, past_b, s_c, w)
    x = layer_norm(DEEPNORM_ALPHA * x + y, w['ln1_g'], w['ln1_b'])
    h = jax.nn.silu(x @ w['w_gate']) * (x @ w['w_up'])
    x = layer_norm(DEEPNORM_ALPHA * x + h @ w['w_down'], w['ln2_g'], w['ln2_b'])
    return x, st


def stack_state(states, i, dtype):
    return jnp.stack([s[i] for s in states], axis=0).astype(dtype)


def setup_inputs(seed: int = 0) -> dict:
    key = jax.random.key(seed)
    ks = jax.random.split(key, 32)
    f32 = jnp.float32

    def nrm(k, shape, s):
        return jax.random.normal(k, shape, f32) * s

    dt = jnp.exp(jax.random.uniform(ks[10], (DEPTH, A_HEADS), f32, math.log(1e-3), math.log(1e-1)))
    return {
        'x_prompt': nrm(ks[0], (BATCH, SEQ, D_MODEL), 1.0),
        'x_sample': nrm(ks[1], (DEC_BATCH, DEC_SEQ, D_MODEL), 1.0),
        'state_a_conv': nrm(ks[2], (DEPTH, DEC_BATCH, CONV_W - 1, A_CONV_DIM), 1.0),
        'state_a_rec': nrm(ks[3], (DEPTH, DEC_BATCH, A_HEADS, A_DK, A_DV), 0.5),
        'cache_b_k': nrm(ks[4], (DEPTH, DEC_BATCH, PAST_LEN, B_HEADS, B_HD), 1.0),
        'cache_b_v': nrm(ks[5], (DEPTH, DEC_BATCH, PAST_LEN, B_HEADS, B_HD), 1.0),
        'cache_b_logf': jax.nn.log_sigmoid(2.0 + nrm(ks[6], (DEPTH, DEC_BATCH, PAST_LEN, B_HEADS), 1.0)),
        'state_c_rec': nrm(ks[7], (DEPTH, DEC_BATCH, C_HEADS, C_DK, C_DV), 0.5),
        'w_in': nrm(ks[8], (DEPTH, D_MODEL, IN_DIM), D_MODEL ** -0.5),
        'conv_w': nrm(ks[9], (DEPTH, CONV_W, A_CONV_DIM), CONV_W ** -0.5),
        'a_log': jnp.log(jax.random.uniform(ks[11], (DEPTH, A_HEADS), f32, 1.0, 16.0)),
        'dt_bias': dt + jnp.log(-jnp.expm1(-dt)),
        'a_norm_g': 1.0 + nrm(ks[12], (DEPTH, A_DV), 0.02),
        'f_bias': 2.0 + nrm(ks[13], (DEPTH, B_HEADS), 0.1),
        'c_w2': nrm(ks[14], (DEPTH, C_RANK, C_KEY), C_RANK ** -0.5),
        'c_b2': nrm(ks[15], (DEPTH, C_KEY), 0.1),
        'c_norm_g': 1.0 + nrm(ks[16], (DEPTH, C_DV), 0.02),
        'w_out': nrm(ks[17], (DEPTH, MIX_WIDTH, D_MODEL), DEEPNORM_BETA * MIX_WIDTH ** -0.5),
        'ln1_g': 1.0 + nrm(ks[18], (DEPTH, D_MODEL), 0.02),
        'ln1_b': nrm(ks[19], (DEPTH, D_MODEL), 0.02),
        'w_gate': nrm(ks[20], (DEPTH, D_MODEL, D_FF), D_MODEL ** -0.5),
        'w_up': nrm(ks[21], (DEPTH, D_MODEL, D_FF), D_MODEL ** -0.5),
        'w_down': nrm(ks[22], (DEPTH, D_FF, D_MODEL), DEEPNORM_BETA * D_FF ** -0.5),
        'ln2_g': 1.0 + nrm(ks[23], (DEPTH, D_MODEL), 0.02),
        'ln2_b': nrm(ks[24], (DEPTH, D_MODEL), 0.02),
    }


def reference(x_prompt, x_sample, state_a_conv, state_a_rec, cache_b_k, cache_b_v, cache_b_logf,
              state_c_rec, w_in, conv_w, a_log, dt_bias, a_norm_g, f_bias, c_w2, c_b2, c_norm_g,
              w_out, ln1_g, ln1_b, w_gate, w_up, w_down, ln2_g, ln2_b):
    f32 = jnp.float32
    nb = x_prompt.shape[0]
    hp, hs = x_prompt, x_sample
    p_states, s_states = [], []
    for l in range(DEPTH):
        w = {'w_in': w_in[l], 'conv_w': conv_w[l], 'a_log': a_log[l], 'dt_bias': dt_bias[l],
             'a_norm_g': a_norm_g[l], 'f_bias': f_bias[l], 'c_w2': c_w2[l], 'c_b2': c_b2[l],
             'c_norm_g': c_norm_g[l], 'w_out': w_out[l], 'ln1_g': ln1_g[l], 'ln1_b': ln1_b[l],
             'w_gate': w_gate[l], 'w_up': w_up[l], 'w_down': w_down[l], 'ln2_g': ln2_g[l],
             'ln2_b': ln2_b[l]}
        hp, stp = trunk_layer(hp,
                              jnp.zeros((nb, CONV_W - 1, A_CONV_DIM), f32),
                              jnp.zeros((nb, A_HEADS, A_DK, A_DV), f32),
                              None,
                              jnp.zeros((nb, C_HEADS, C_DK, C_DV), f32), w)
        p_states.append(stp)
        hs, sts = trunk_layer(hs, state_a_conv[l], state_a_rec[l],
                              (cache_b_k[l], cache_b_v[l], cache_b_logf[l]), state_c_rec[l], w)
        s_states.append(sts)
    dp, ds = x_prompt.dtype, x_sample.dtype
    return (hp, hs,
            stack_state(p_states, 0, dp), stack_state(p_states, 1, dp), stack_state(p_states, 2, dp),
            stack_state(p_states, 3, dp), stack_state(p_states, 4, dp), stack_state(p_states, 5, dp),
            stack_state(s_states, 0, ds), stack_state(s_states, 1, ds), stack_state(s_states, 2, ds),
            stack_state(s_states, 3, ds), stack_state(s_states, 4, ds), stack_state(s_states, 5, ds))
```

```python
import functools

import jax
import jax.numpy as jnp
from jax import lax
from jax.experimental import pallas as pl
from jax.experimental.pallas import tpu as pltpu

F32 = jnp.float32
BF16 = jnp.bfloat16

D_MODEL = 4096
CHUNK = 64
HEAD_DIM = 128
A_HEADS = 12
A_KEY = A_HEADS * HEAD_DIM
A_VAL = A_HEADS * HEAD_DIM
A_CONV_DIM = 2 * A_KEY + A_VAL
CONV_W = 4
B_HEADS = 8
B_WIDTH = B_HEADS * HEAD_DIM
C_HEADS = 4
C_DV = 384
C_DK = 192
C_DK_PAD = 256
C_KEY = C_HEADS * C_DK
C_KEY_PAD = C_HEADS * C_DK_PAD
C_VAL = C_HEADS * C_DV
C_RANK = 16
C_TAU = 16.0
D_FF = 11008
DEPTH_ALPHA_POW = 0.25
LN_EPS = 1e-5
RMS_EPS = 1e-6

LANES = 128
SUBLANES = 8
VMEM_LIMIT = 56 * 1024 * 1024

COL_QKV_A = 0
COL_Z_A = 4608
COL_V_C = 6144
COL_R_C = 7680
COL_Q_B = 9216
COL_K_B = 10240
COL_V_B = 11264
COL_Q_C = 12288
COL_K_C = 13312
IN_BIG = 14336
GATE_BLOCKS = 4
IN_GATE = GATE_BLOCKS * LANES

NEG_BIG = -1e30


def _cparams(sem):
    return pltpu.CompilerParams(dimension_semantics=sem, vmem_limit_bytes=VMEM_LIMIT)


def _sigmoid(x):
    return 1.0 / (1.0 + jnp.exp(-x))


def _silu(x):
    return x * _sigmoid(x)


def _softplus(x):
    return jnp.maximum(x, 0.0) + jnp.log(1.0 + jnp.exp(-jnp.abs(x)))


def _log_sigmoid(x):
    return -_softplus(-x)


def _split3(x):
    x1 = x.astype(BF16)
    r1 = x - x1.astype(F32)
    x2 = r1.astype(BF16)
    x3 = (r1 - x2.astype(F32)).astype(BF16)
    return x1, x2, x3


def _dot(a, b):
    return jnp.dot(a, b, preferred_element_type=F32)


def _dot_nt(a, b):
    return lax.dot_general(a, b, (((1,), (1,)), ((), ())), preferred_element_type=F32)


def _dot_tn(a, b):
    return lax.dot_general(a, b, (((0,), (0,)), ((), ())), preferred_element_type=F32)


def _dot_exact_lhs(a_bf16, x):
    x1, x2, x3 = _split3(x)
    return _dot(a_bf16, x1) + _dot(a_bf16, x2) + _dot(a_bf16, x3)


def _dot3(a, b):
    a1 = a.astype(BF16)
    a2 = (a - a1.astype(F32)).astype(BF16)
    b1 = b.astype(BF16)
    b2 = (b - b1.astype(F32)).astype(BF16)
    return _dot(a1, b1) + _dot(a1, b2) + _dot(a2, b1)


def _iota2(shape, dim):
    return lax.broadcasted_iota(jnp.int32, shape, dim)


def _tri_incl(n):
    return (_iota2((n, n), 1) <= _iota2((n, n), 0)).astype(BF16)


def _mm_kernel(x_ref, w_ref, o_ref):
    o_ref[...] = _dot(x_ref[...], w_ref[...]).astype(o_ref.dtype)


def _matmul(x, w, tm, tn, out_dtype):
    m, k = x.shape
    n = w.shape[1]
    tm = min(tm, m)
    return pl.pallas_call(
        _mm_kernel,
        grid=(m // tm, n // tn),
        in_specs=[pl.BlockSpec((tm, k), lambda i, j: (i, 0)),
                  pl.BlockSpec((k, tn), lambda i, j: (0, j))],
        out_specs=pl.BlockSpec((tm, tn), lambda i, j: (i, j)),
        out_shape=jax.ShapeDtypeStruct((m, n), out_dtype),
        compiler_params=_cparams(("parallel", "parallel")),
    )(x, w)


def _swiglu_kernel(x_ref, wg_ref, wu_ref, o_ref):
    x = x_ref[...]
    a = _dot(x, wg_ref[...])
    b = _dot(x, wu_ref[...])
    o_ref[...] = (_silu(a) * b).astype(o_ref.dtype)


def _swiglu(x, wg, wu, tm, tn):
    m, k = x.shape
    n = wg.shape[1]
    tm = min(tm, m)
    return pl.pallas_call(
        _swiglu_kernel,
        grid=(m // tm, n // tn),
        in_specs=[pl.BlockSpec((tm, k), lambda i, j: (i, 0)),
                  pl.BlockSpec((k, tn), lambda i, j: (0, j)),
                  pl.BlockSpec((k, tn), lambda i, j: (0, j))],
        out_specs=pl.BlockSpec((tm, tn), lambda i, j: (i, j)),
        out_shape=jax.ShapeDtypeStruct((m, n), BF16),
        compiler_params=_cparams(("parallel", "parallel")),
    )(x, wg, wu)


def _ln_kernel(alpha, x_ref, y_ref, g_ref, b_ref, o_ref, ob_ref):
    v = alpha * x_ref[...] + y_ref[...]
    mu = jnp.mean(v, axis=-1, keepdims=True)
    d = v - mu
    var = jnp.mean(d * d, axis=-1, keepdims=True)
    out = d * lax.rsqrt(var + LN_EPS) * g_ref[...] + b_ref[...]
    o_ref[...] = out
    ob_ref[...] = out.astype(BF16)


def _residual_ln(x, y, g, b, alpha, tm=256):
    m, d = x.shape
    tm = min(tm, m)
    row = pl.BlockSpec((tm, d), lambda i: (i, 0))
    vec = pl.BlockSpec((1, d), lambda i: (0, 0))
    return pl.pallas_call(
        functools.partial(_ln_kernel, alpha),
        grid=(m // tm,),
        in_specs=[row, row, vec, vec],
        out_specs=[row, row],
        out_shape=[jax.ShapeDtypeStruct((m, d), F32), jax.ShapeDtypeStruct((m, d), BF16)],
        compiler_params=_cparams(("parallel",)),
    )(x, y, g.reshape(1, d), b.reshape(1, d))


def _fgate_kernel(apply_gate, tb, x_ref, bias_ref, c0_ref, lf_ref, c_ref, carry):
    @pl.when(pl.program_id(1) == 0)
    def _():
        carry[...] = c0_ref[0]

    x = x_ref[0]
    lf = _log_sigmoid(x + bias_ref[...]) if apply_gate else x
    c = _dot_exact_lhs(_tri_incl(tb), lf) + carry[...]
    lf_ref[0] = lf
    c_ref[0] = c
    carry[...] = c[tb - 1:tb, :]


def _fgate(x, col_block, bias, c0, apply_gate, tb):
    n, t, _ = x.shape
    tb = min(tb, t)
    blk = pl.BlockSpec((1, tb, LANES), lambda i, j: (i, j, 0))
    return pl.pallas_call(
        functools.partial(_fgate_kernel, apply_gate, tb),
        grid=(n, t // tb),
        in_specs=[pl.BlockSpec((1, tb, LANES), lambda i, j: (i, j, col_block)),
                  pl.BlockSpec((1, LANES), lambda i, j: (0, 0)),
                  pl.BlockSpec((1, 1, LANES), lambda i, j: (i, 0, 0))],
        out_specs=[blk, blk],
        out_shape=[jax.ShapeDtypeStruct((n, t, LANES), F32)] * 2,
        scratch_shapes=[pltpu.VMEM((1, LANES), F32)],
        compiler_params=_cparams(("parallel", "arbitrary")),
    )(x, bias, c0)


def _gdn_kernel(L, qkv_ref, z_ref, gate_ref, buf_ref, convw_ref, alog_ref, dtb_ref, ng_ref, s0_ref,
                o_ref, s_ref, xs):
    @pl.when(pl.program_id(1) == 0)
    def _():
        xs[0:SUBLANES, :] = buf_ref[0]
        s_ref[0] = s0_ref[0]

    xs[SUBLANES:SUBLANES + L, :] = qkv_ref[0]

    def conv_cols(c0):
        acc = xs[5:5 + L, c0:c0 + HEAD_DIM] * convw_ref[0:1, c0:c0 + HEAD_DIM]
        for i in range(1, CONV_W):
            acc = acc + xs[5 + i:5 + i + L, c0:c0 + HEAD_DIM] * convw_ref[i:i + 1, c0:c0 + HEAD_DIM]
        return _silu(acc)

    def l2n(x):
        return x * lax.rsqrt(jnp.sum(x * x, axis=-1, keepdims=True) + RMS_EPS)

    a_in = gate_ref[0, :, 0:LANES]
    b_in = gate_ref[0, :, LANES:2 * LANES]
    g = -jnp.exp(alog_ref[...]) * _softplus(a_in + dtb_ref[...])
    beta = _sigmoid(b_in)
    gc = _dot_exact_lhs(_tri_incl(L), g)
    eye_l = (_iota2((LANES, LANES), 0) == _iota2((LANES, LANES), 1)).astype(BF16)
    g1, g2, g3 = _split3(gc)
    gct = _dot_nt(eye_l, g1) + _dot_nt(eye_l, g2) + _dot_nt(eye_l, g3)

    row = _iota2((L, L), 0)
    col = _iota2((L, L), 1)
    eye = (row == col).astype(F32)
    n_sq = L.bit_length() - 2

    for h in range(A_HEADS):
        q = l2n(conv_cols(h * HEAD_DIM)) * (HEAD_DIM ** -0.5)
        k = l2n(conv_cols(A_KEY + h * HEAD_DIM))
        v = conv_cols(2 * A_KEY + h * HEAD_DIM)
        gcol = gc[:, h:h + 1]
        grow = gct[h:h + 1, :]
        bcol = beta[:, h:h + 1]
        glast = gc[L - 1:L, h:h + 1]
        dec = jnp.exp(jnp.where(row >= col, gcol - grow, NEG_BIG))
        dec_strict = jnp.where(row > col, dec, 0.0)
        eg = jnp.exp(gcol)

        qb = q.astype(BF16)
        kb = k.astype(BF16)
        kk = _dot_nt(kb, kb)
        qk = _dot_nt(qb, kb)
        p = -(bcol * kk * dec_strict)
        t_inv = eye + p
        for _ in range(n_sq):
            p = _dot3(p, p)
            t_inv = t_inv + _dot3(p, t_inv)
        rhs = jnp.concatenate([(bcol * eg) * k, bcol * v], axis=1)
        tr = _dot3(t_inv, rhs)
        w = tr[:, 0:HEAD_DIM]
        u0 = tr[:, HEAD_DIM:2 * HEAD_DIM]

        s = s_ref[0, h]
        sb = s.astype(BF16)
        u = u0 - _dot(w.astype(BF16), sb)
        ub = u.astype(BF16)
        o = eg * _dot(qb, sb) + _dot((qk * dec).astype(BF16), ub)
        k_end = k * jnp.exp(glast - gcol)
        s_ref[0, h] = jnp.exp(glast) * s + _dot_tn(k_end.astype(BF16), ub)

        o = o * lax.rsqrt(jnp.mean(o * o, axis=-1, keepdims=True) + RMS_EPS) * ng_ref[...]
        zh = z_ref[0, :, h * HEAD_DIM:(h + 1) * HEAD_DIM]
        o_ref[0, :, h * HEAD_DIM:(h + 1) * HEAD_DIM] = (o * _silu(zh)).astype(BF16)

    tail = xs[L:L + SUBLANES, :]
    xs[0:SUBLANES, :] = tail


def _gdn(hbig, hgate, conv_buf8, conv_w, a_log, dt_bias, norm_g, s0):
    n, t, _ = hbig.shape
    L = min(t, CHUNK)
    const2 = lambda i, j: (0, 0)
    state = pl.BlockSpec((1, A_HEADS, HEAD_DIM, HEAD_DIM), lambda i, j: (i, 0, 0, 0))
    return pl.pallas_call(
        functools.partial(_gdn_kernel, L),
        grid=(n, t // L),
        in_specs=[pl.BlockSpec((1, L, A_CONV_DIM), lambda i, j: (i, j, COL_QKV_A // A_CONV_DIM)),
                  pl.BlockSpec((1, L, A_VAL), lambda i, j: (i, j, COL_Z_A // A_VAL)),
                  pl.BlockSpec((1, L, 2 * LANES), lambda i, j: (i, j, 0)),
                  pl.BlockSpec((1, SUBLANES, A_CONV_DIM), lambda i, j: (i, 0, 0)),
                  pl.BlockSpec((CONV_W, A_CONV_DIM), const2),
                  pl.BlockSpec((1, LANES), const2),
                  pl.BlockSpec((1, LANES), const2),
                  pl.BlockSpec((1, HEAD_DIM), const2),
                  state],
        out_specs=[pl.BlockSpec((1, L, A_VAL), lambda i, j: (i, j, 0)), state],
        out_shape=[jax.ShapeDtypeStruct((n, t, A_VAL), BF16),
                   jax.ShapeDtypeStruct((n, A_HEADS, HEAD_DIM, HEAD_DIM), F32)],
        scratch_shapes=[pltpu.VMEM((L + SUBLANES, A_CONV_DIM), F32)],
        compiler_params=_cparams(("parallel", "arbitrary")),
    )(hbig, hbig, hgate, conv_buf8, conv_w, a_log, dt_bias, norm_g, s0)


def _gla_kernel(L, q_ref, k_ref, v_ref, r_ref, gate_ref, w2_ref, b2_ref, ng_ref, s0_ref,
                o_ref, s_ref):
    @pl.when(pl.program_id(1) == 0)
    def _():
        s_ref[0] = s0_ref[0]

    z = _dot(gate_ref[0].astype(BF16), w2_ref[...]) + b2_ref[...]
    lg = _log_sigmoid(z) * (1.0 / C_TAU)
    b = _dot_exact_lhs(_tri_incl(L), lg)
    causal = _iota2((L, L), 0) >= _iota2((L, L), 1)

    for h in range(C_HEADS):
        ks = slice(h * C_DK_PAD, (h + 1) * C_DK_PAD)
        vs = slice(h * C_DV, (h + 1) * C_DV)
        bh = b[:, ks]
        blast = bh[L - 1:L, :]
        bref = bh[L // 2:L // 2 + 1, :]
        q = q_ref[0, :, ks] * (C_DK ** -0.5)
        k = k_ref[0, :, ks]
        vb = v_ref[0, :, vs].astype(BF16)
        st = s_ref[0, h]
        o = _dot_nt((q * jnp.exp(bh)).astype(BF16), st.astype(BF16))
        att = _dot_nt((q * jnp.exp(bh - bref)).astype(BF16), (k * jnp.exp(bref - bh)).astype(BF16))
        att = jnp.where(causal, att, 0.0)
        o = o + _dot(att.astype(BF16), vb)
        s_ref[0, h] = st * jnp.exp(blast) + _dot_tn(vb, (k * jnp.exp(blast - bh)).astype(BF16))

        o = o * lax.rsqrt(jnp.mean(o * o, axis=-1, keepdims=True) + RMS_EPS) * ng_ref[...]
        o_ref[0, :, vs] = (o * _silu(r_ref[0, :, vs])).astype(BF16)


def _gla(hbig, hgate, w2p, b2p, norm_g, s0t):
    n, t, _ = hbig.shape
    L = min(t, CHUNK)
    const2 = lambda i, j: (0, 0)
    state = pl.BlockSpec((1, C_HEADS, C_DV, C_DK_PAD), lambda i, j: (i, 0, 0, 0))
    return pl.pallas_call(
        functools.partial(_gla_kernel, L),
        grid=(n, t // L),
        in_specs=[pl.BlockSpec((1, L, C_KEY_PAD), lambda i, j: (i, j, COL_Q_C // C_KEY_PAD)),
                  pl.BlockSpec((1, L, C_KEY_PAD), lambda i, j: (i, j, COL_K_C // C_KEY_PAD)),
                  pl.BlockSpec((1, L, C_VAL), lambda i, j: (i, j, COL_V_C // C_VAL)),
                  pl.BlockSpec((1, L, C_VAL), lambda i, j: (i, j, COL_R_C // C_VAL)),
                  pl.BlockSpec((1, L, LANES), lambda i, j: (i, j, 3)),
                  pl.BlockSpec((LANES, C_KEY_PAD), const2),
                  pl.BlockSpec((1, C_KEY_PAD), const2),
                  pl.BlockSpec((1, C_DV), const2),
                  state],
        out_specs=[pl.BlockSpec((1, L, C_VAL), lambda i, j: (i, j, 0)), state],
        out_shape=[jax.ShapeDtypeStruct((n, t, C_VAL), BF16),
                   jax.ShapeDtypeStruct((n, C_HEADS, C_DV, C_DK_PAD), F32)],
        compiler_params=_cparams(("parallel", "arbitrary")),
    )(hbig, hbig, hbig, hbig, hgate, w2p, b2p, norm_g, s0t)


def _fox_prompt_kernel(tq, tk, q_ref, k_ref, v_ref, cq_ref, ck_ref, o_ref, m_sc, l_sc, acc_sc):
    qi = pl.program_id(1)
    ki = pl.program_id(2)

    @pl.when(ki == 0)
    def _():
        m_sc[...] = jnp.full(m_sc.shape, NEG_BIG, F32)
        l_sc[...] = jnp.zeros(l_sc.shape, F32)
        acc_sc[...] = jnp.zeros(acc_sc.shape, F32)

    @pl.when(ki <= qi)
    def _():
        visible = (ki * tk + _iota2((tq, tk), 1)) <= (qi * tq + _iota2((tq, tk), 0))
        for h in range(B_HEADS):
            hs = slice(h * HEAD_DIM, (h + 1) * HEAD_DIM)
            qb = (q_ref[0, :, hs] * (HEAD_DIM ** -0.5)).astype(BF16)
            kb = k_ref[0, :, hs].astype(BF16)
            s = _dot_nt(qb, kb) + (cq_ref[0, :, h:h + 1] - ck_ref[0, h:h + 1, :])
            s = jnp.where(visible, s, NEG_BIG)
            m_prev = m_sc[h]
            m_new = jnp.maximum(m_prev, jnp.max(s, axis=-1, keepdims=True))
            p = jnp.exp(s - m_new)
            alpha = jnp.exp(m_prev - m_new)
            l_sc[h] = alpha * l_sc[h] + jnp.sum(p, axis=-1, keepdims=True)
            acc_sc[h] = alpha * acc_sc[h] + _dot(p.astype(BF16), v_ref[0, :, hs].astype(BF16))
            m_sc[h] = m_new

    @pl.when(ki == qi)
    def _():
        for h in range(B_HEADS):
            o_ref[0, :, h * HEAD_DIM:(h + 1) * HEAD_DIM] = (acc_sc[h] / l_sc[h]).astype(BF16)


def _fox_prompt(hbig, c, ct, tq=512):
    n, t, _ = hbig.shape
    tq = min(tq, t)
    tk = tq
    nq = t // tq
    kv_idx = lambda blk: (lambda i, qi, ki: (i, jnp.minimum(ki, qi), blk))
    return pl.pallas_call(
        functools.partial(_fox_prompt_kernel, tq, tk),
        grid=(n, nq, nq),
        in_specs=[pl.BlockSpec((1, tq, B_WIDTH), lambda i, qi, ki: (i, qi, COL_Q_B // B_WIDTH)),
                  pl.BlockSpec((1, tk, B_WIDTH), kv_idx(COL_K_B // B_WIDTH)),
                  pl.BlockSpec((1, tk, B_WIDTH), kv_idx(COL_V_B // B_WIDTH)),
                  pl.BlockSpec((1, tq, LANES), lambda i, qi, ki: (i, qi, 0)),
                  pl.BlockSpec((1, SUBLANES, tk), lambda i, qi, ki: (i, 0, jnp.minimum(ki, qi)))],
        out_specs=pl.BlockSpec((1, tq, B_WIDTH), lambda i, qi, ki: (i, qi, 0)),
        out_shape=jax.ShapeDtypeStruct((n, t, B_WIDTH), BF16),
        scratch_shapes=[pltpu.VMEM((B_HEADS, tq, 1), F32),
                        pltpu.VMEM((B_HEADS, tq, 1), F32),
                        pltpu.VMEM((B_HEADS, tq, HEAD_DIM), F32)],
        compiler_params=_cparams(("parallel", "parallel", "arbitrary")),
    )(hbig, hbig, hbig, c, ct)


def _fox_sample_kernel(t, q_ref, kn_ref, vn_ref, kp_ref, vp_ref, cq_ref, ckp_ref, ckn_ref, o_ref):
    h = pl.program_id(1)
    qb = (q_ref[0] * (HEAD_DIM ** -0.5)).astype(BF16)
    cq = cq_ref[0, 0]
    sp = _dot_nt(qb, kp_ref[0].astype(BF16)) + (cq - ckp_ref[0, 0])
    sn = _dot_nt(qb, kn_ref[0].astype(BF16)) + (cq - ckn_ref[0, 0])
    sn = jnp.where(_iota2((t, t), 1) <= _iota2((t, t), 0), sn, NEG_BIG)
    m = jnp.maximum(jnp.max(sp, axis=-1, keepdims=True), jnp.max(sn, axis=-1, keepdims=True))
    pp = jnp.exp(sp - m)
    pn = jnp.exp(sn - m)
    den = jnp.sum(pp, axis=-1, keepdims=True) + jnp.sum(pn, axis=-1, keepdims=True)
    o = _dot(pp.astype(BF16), vp_ref[0].astype(BF16)) + _dot(pn.astype(BF16), vn_ref[0].astype(BF16))
    o_ref[0] = (o / den).astype(BF16)
    del h


def _fox_sample(hbig, k_past, v_past, cq, ckp, ckn):
    n, t, _ = hbig.shape
    p = k_past.shape[1]
    head_col = lambda base: (lambda i, h: (i, 0, base // HEAD_DIM + h))
    return pl.pallas_call(
        functools.partial(_fox_sample_kernel, t),
        grid=(n, B_HEADS),
        in_specs=[pl.BlockSpec((1, t, HEAD_DIM), head_col(COL_Q_B)),
                  pl.BlockSpec((1, t, HEAD_DIM), head_col(COL_K_B)),
                  pl.BlockSpec((1, t, HEAD_DIM), head_col(COL_V_B)),
                  pl.BlockSpec((1, p, HEAD_DIM), lambda i, h: (i, 0, h)),
                  pl.BlockSpec((1, p, HEAD_DIM), lambda i, h: (i, 0, h)),
                  pl.BlockSpec((1, 1, t, 1), lambda i, h: (i, h, 0, 0)),
                  pl.BlockSpec((1, 1, 1, p), lambda i, h: (i, h, 0, 0)),
                  pl.BlockSpec((1, 1, 1, t), lambda i, h: (i, h, 0, 0))],
        out_specs=pl.BlockSpec((1, t, HEAD_DIM), lambda i, h: (i, 0, h)),
        out_shape=jax.ShapeDtypeStruct((n, t, B_WIDTH), BF16),
        compiler_params=_cparams(("parallel", "parallel")),
    )(hbig, hbig, hbig, k_past, v_past, cq, ckp, ckn)


def _pad_cols(w, width):
    return jnp.pad(w, ((0, 0), (0, width - w.shape[1])))


def _pad_heads_c(w):
    lead = w.shape[:-1]
    w = w.reshape(lead + (C_HEADS, C_DK))
    w = jnp.pad(w, [(0, 0)] * len(lead) + [(0, 0), (0, C_DK_PAD - C_DK)])
    return w.reshape(lead + (C_KEY_PAD,))


def _prep_w_in(w):
    o = 0
    parts = {}
    for name, size in (("qkv_a", A_CONV_DIM), ("z_a", A_VAL), ("a_a", A_HEADS), ("b_a", A_HEADS),
                       ("q_b", B_WIDTH), ("k_b", B_WIDTH), ("v_b", B_WIDTH), ("f_b", B_HEADS),
                       ("q_c", C_KEY), ("k_c", C_KEY), ("v_c", C_VAL), ("r_c", C_VAL), ("lr_c", C_RANK)):
        parts[name] = w[:, o:o + size]
        o += size
    big = jnp.concatenate([parts["qkv_a"], parts["z_a"], parts["v_c"], parts["r_c"],
                           parts["q_b"], parts["k_b"], parts["v_b"],
                           _pad_heads_c(parts["q_c"]), _pad_heads_c(parts["k_c"])], axis=1)
    gate = jnp.concatenate([_pad_cols(parts[nm], LANES) for nm in ("a_a", "b_a", "f_b", "lr_c")], axis=1)
    return big.astype(BF16), gate.astype(BF16)


def _pad_vec(v):
    return _pad_cols(v.reshape(1, -1).astype(F32), LANES)


def _mixer(xb, n, t, conv_buf8, s_a0, past_b, s_c0t, w):
    m = n * t
    hbig = _matmul(xb, w["w_big"], 1024, 1024, F32).reshape(n, t, IN_BIG)
    hgate = _matmul(xb, w["w_gate4"], 1024, IN_GATE, F32).reshape(n, t, IN_GATE)

    oa, s_a = _gdn(hbig, hgate, conv_buf8, w["conv_w"], w["a_log"], w["dt_bias"], w["a_norm_g"], s_a0)
    oc, s_ct = _gla(hbig, hgate, w["c_w2p"], w["c_b2p"], w["c_norm_g"], s_c0t)

    zero_c = jnp.zeros((n, 1, LANES), F32)
    if past_b is None:
        lf, c = _fgate(hgate, 2, w["f_bias"], zero_c, True, 256)
        ob = _fox_prompt(hbig, c, jnp.swapaxes(c[:, :, :SUBLANES], 1, 2))
    else:
        k_past, v_past, lf_past = past_b
        p = k_past.shape[1]
        lf_past = jnp.pad(lf_past.astype(F32), ((0, 0), (0, 0), (0, LANES - B_HEADS)))
        _, c_past = _fgate(lf_past, 0, w["f_bias"], zero_c, False, 256)
        lf, c = _fgate(hgate, 2, w["f_bias"], c_past[:, p - 1:p, :], True, t)
        cq = jnp.swapaxes(c[:, :, :B_HEADS], 1, 2)[..., None]
        ckn = jnp.swapaxes(c[:, :, :B_HEADS], 1, 2)[:, :, None, :]
        ckp = jnp.swapaxes(c_past[:, :, :B_HEADS], 1, 2)[:, :, None, :]
        ob = _fox_sample(hbig, k_past.reshape(n, p, B_WIDTH).astype(F32),
                         v_past.reshape(n, p, B_WIDTH).astype(F32), cq, ckp, ckn)

    o = jnp.concatenate([oa, ob, oc], axis=-1).reshape(m, D_MODEL)
    y = _matmul(o, w["w_out"], 1024, 1024, F32)

    conv_new = hbig[:, t - (CONV_W - 1):, COL_QKV_A:COL_QKV_A + A_CONV_DIM]
    kb = hbig[:, :, COL_K_B:COL_K_B + B_WIDTH].reshape(n, t, B_HEADS, HEAD_DIM)
    vb = hbig[:, :, COL_V_B:COL_V_B + B_WIDTH].reshape(n, t, B_HEADS, HEAD_DIM)
    s_c = jnp.swapaxes(s_ct, 2, 3)[:, :, :C_DK, :]
    return y, (conv_new, s_a, kb, vb, lf[:, :, :B_HEADS], s_c)


def _layer(x, xb, n, t, conv_buf8, s_a0, past_b, s_c0t, w, alpha):
    y, st = _mixer(xb, n, t, conv_buf8, s_a0, past_b, s_c0t, w)
    x, xb = _residual_ln(x, y, w["ln1_g"], w["ln1_b"], alpha)
    hmid = _swiglu(xb, w["w_gate"], w["w_up"], 1024, 256)
    y2 = _matmul(hmid, w["w_down"], 512, 512, F32)
    x, xb = _residual_ln(x, y2, w["ln2_g"], w["ln2_b"], alpha)
    return x, xb, st


def kernel(x_prompt, x_sample, state_a_conv, state_a_rec, cache_b_k, cache_b_v, cache_b_logf, state_c_rec, w_in, conv_w, a_log, dt_bias, a_norm_g, f_bias, c_w2, c_b2, c_norm_g, w_out, ln1_g, ln1_b, w_gate, w_up, w_down, ln2_g, ln2_b):
    depth = w_in.shape[0]
    alpha = (2 * depth) ** DEPTH_ALPHA_POW
    nb, tp, _ = x_prompt.shape
    ns, ts, _ = x_sample.shape

    hp = x_prompt.reshape(nb * tp, D_MODEL).astype(F32)
    hs = x_sample.reshape(ns * ts, D_MODEL).astype(F32)
    hpb = hp.astype(BF16)
    hsb = hs.astype(BF16)
    p_states, s_states = [], []
    for l in range(depth):
        w_big, w_gate4 = _prep_w_in(w_in[l])
        w = {
            "w_big": w_big, "w_gate4": w_gate4,
            "conv_w": conv_w[l].astype(F32),
            "a_log": _pad_vec(a_log[l]), "dt_bias": _pad_vec(dt_bias[l]),
            "a_norm_g": a_norm_g[l].reshape(1, HEAD_DIM).astype(F32),
            "f_bias": _pad_vec(f_bias[l]),
            "c_w2p": jnp.pad(_pad_heads_c(c_w2[l]), ((0, LANES - C_RANK), (0, 0))).astype(BF16),
            "c_b2p": _pad_heads_c(c_b2[l].reshape(1, C_KEY)).astype(F32),
            "c_norm_g": c_norm_g[l].reshape(1, C_DV).astype(F32),
            "w_out": w_out[l].astype(BF16),
            "ln1_g": ln1_g[l], "ln1_b": ln1_b[l], "ln2_g": ln2_g[l], "ln2_b": ln2_b[l],
            "w_gate": w_gate[l].astype(BF16), "w_up": w_up[l].astype(BF16),
            "w_down": w_down[l].astype(BF16),
        }
        hp, hpb, stp = _layer(
            hp, hpb, nb, tp,
            jnp.zeros((nb, SUBLANES, A_CONV_DIM), F32),
            jnp.zeros((nb, A_HEADS, HEAD_DIM, HEAD_DIM), F32),
            None,
            jnp.zeros((nb, C_HEADS, C_DV, C_DK_PAD), F32), w, alpha)
        p_states.append(stp)
        buf8 = jnp.pad(state_a_conv[l].astype(F32), ((0, 0), (SUBLANES - (CONV_W - 1), 0), (0, 0)))
        s_c0t = jnp.pad(jnp.swapaxes(state_c_rec[l].astype(F32), 2, 3),
                        ((0, 0), (0, 0), (0, 0), (0, C_DK_PAD - C_DK)))
        hs, hsb, sts = _layer(
            hs, hsb, ns, ts, buf8, state_a_rec[l].astype(F32),
            (cache_b_k[l], cache_b_v[l], cache_b_logf[l]), s_c0t, w, alpha)
        s_states.append(sts)

    dp, ds = x_prompt.dtype, x_sample.dtype
    stack = lambda states, i, dt: jnp.stack([s[i] for s in states], axis=0).astype(dt)
    return ((hp.reshape(nb, tp, D_MODEL).astype(dp), hs.reshape(ns, ts, D_MODEL).astype(ds))
            + tuple(stack(p_states, i, dp) for i in range(6))
            + tuple(stack(s_states, i, ds) for i in range(6)))
```

```python
import functools

import jax
import jax.numpy as jnp
from jax import lax
from jax.experimental import pallas as pl
from jax.experimental.pallas import tpu as pltpu

F32 = jnp.float32
BF16 = jnp.bfloat16

D_MODEL = 4096
CHUNK = 64
HEAD_DIM = 128
A_HEADS = 12
A_KEY = A_HEADS * HEAD_DIM
A_VAL = A_HEADS * HEAD_DIM
A_CONV_DIM = 2 * A_KEY + A_VAL
CONV_W = 4
B_HEADS = 8
B_WIDTH = B_HEADS * HEAD_DIM
C_HEADS = 4
C_DV = 384
C_DK = 192
C_DK_PAD = 256
C_KEY = C_HEADS * C_DK
C_KEY_PAD = C_HEADS * C_DK_PAD
C_VAL = C_HEADS * C_DV
C_RANK = 16
C_TAU = 16.0
D_FF = 11008
DEPTH_ALPHA_POW = 0.25
LN_EPS = 1e-5
RMS_EPS = 1e-6

LANES = 128
SUBLANES = 8
VMEM_LIMIT = 56 * 1024 * 1024

COL_QKV_A = 0
COL_Z_A = 4608
COL_V_C = 6144
COL_R_C = 7680
COL_Q_B = 9216
COL_K_B = 10240
COL_V_B = 11264
COL_Q_C = 12288
COL_K_C = 13312
IN_BIG = 14336
GATE_BLOCKS = 4
IN_GATE = GATE_BLOCKS * LANES

NEG_BIG = -1e30
LOG2E = 1.4426950408889634


def _cparams(sem):
    return pltpu.CompilerParams(dimension_semantics=sem, vmem_limit_bytes=VMEM_LIMIT)


def _sigmoid(x):
    return 1.0 / (1.0 + jnp.exp(-x))


def _silu(x):
    return x * _sigmoid(x)


def _softplus(x):
    return jnp.maximum(x, 0.0) + jnp.log(1.0 + jnp.exp(-jnp.abs(x)))


def _log_sigmoid(x):
    return -_softplus(-x)


def _split3(x):
    x1 = x.astype(BF16)
    r1 = x - x1.astype(F32)
    x2 = r1.astype(BF16)
    x3 = (r1 - x2.astype(F32)).astype(BF16)
    return x1, x2, x3


def _dot(a, b):
    return jnp.dot(a, b, preferred_element_type=F32)


def _dot_nt(a, b):
    return lax.dot_general(a, b, (((1,), (1,)), ((), ())), preferred_element_type=F32)


def _dot_tn(a, b):
    return lax.dot_general(a, b, (((0,), (0,)), ((), ())), preferred_element_type=F32)


def _dot_exact_lhs(a_bf16, x):
    x1, x2, x3 = _split3(x)
    return _dot(a_bf16, x1) + _dot(a_bf16, x2) + _dot(a_bf16, x3)


def _dot3(a, b):
    a1 = a.astype(BF16)
    a2 = (a - a1.astype(F32)).astype(BF16)
    b1 = b.astype(BF16)
    b2 = (b - b1.astype(F32)).astype(BF16)
    return _dot(a1, b1) + _dot(a1, b2) + _dot(a2, b1)


def _iota2(shape, dim):
    return lax.broadcasted_iota(jnp.int32, shape, dim)


def _tri_incl(n):
    return (_iota2((n, n), 1) <= _iota2((n, n), 0)).astype(BF16)


def _mm_kernel(x_ref, w_ref, o_ref):
    o_ref[...] = _dot(x_ref[...], w_ref[...]).astype(o_ref.dtype)


def _matmul(x, w, layer, tm, tn, out_dtype):
    m, k = x.shape
    n = w.shape[2]
    tm = min(tm, m)
    return pl.pallas_call(
        _mm_kernel,
        grid=(m // tm, n // tn),
        in_specs=[pl.BlockSpec((tm, k), lambda i, j: (i, 0)),
                  pl.BlockSpec((None, k, tn), lambda i, j: (layer, 0, j))],
        out_specs=pl.BlockSpec((tm, tn), lambda i, j: (i, j)),
        out_shape=jax.ShapeDtypeStruct((m, n), out_dtype),
        compiler_params=_cparams(("parallel", "parallel")),
    )(x, w)


def _outproj_kernel(a_ref, b_ref, c_ref, w_ref, o_ref):
    o_ref[...] = (_dot(a_ref[...], w_ref[0:A_VAL, :])
                  + _dot(b_ref[...], w_ref[A_VAL:A_VAL + B_WIDTH, :])
                  + _dot(c_ref[...], w_ref[A_VAL + B_WIDTH:D_MODEL, :]))


def _outproj(oa, ob, oc, w, layer, tm, tn):
    m = oa.shape[0]
    tm = min(tm, m)
    rows = lambda width: pl.BlockSpec((tm, width), lambda i, j: (i, 0))
    return pl.pallas_call(
        _outproj_kernel,
        grid=(m // tm, D_MODEL // tn),
        in_specs=[rows(A_VAL), rows(B_WIDTH), rows(C_VAL),
                  pl.BlockSpec((None, D_MODEL, tn), lambda i, j: (layer, 0, j))],
        out_specs=pl.BlockSpec((tm, tn), lambda i, j: (i, j)),
        out_shape=jax.ShapeDtypeStruct((m, D_MODEL), F32),
        compiler_params=_cparams(("parallel", "parallel")),
    )(oa, ob, oc, w)


def _swiglu_kernel(x_ref, wg_ref, wu_ref, o_ref):
    x = x_ref[...]
    a = _dot(x, wg_ref[...])
    b = _dot(x, wu_ref[...])
    o_ref[...] = (_silu(a) * b).astype(o_ref.dtype)


def _swiglu(x, wg, wu, layer, tm, tn):
    m, k = x.shape
    n = wg.shape[2]
    tm = min(tm, m)
    wspec = pl.BlockSpec((None, k, tn), lambda i, j: (layer, 0, j))
    return pl.pallas_call(
        _swiglu_kernel,
        grid=(m // tm, n // tn),
        in_specs=[pl.BlockSpec((tm, k), lambda i, j: (i, 0)), wspec, wspec],
        out_specs=pl.BlockSpec((tm, tn), lambda i, j: (i, j)),
        out_shape=jax.ShapeDtypeStruct((m, n), BF16),
        compiler_params=_cparams(("parallel", "parallel")),
    )(x, wg, wu)


def _ln_kernel(alpha, x_ref, y_ref, g_ref, b_ref, o_ref, ob_ref):
    v = alpha * x_ref[...] + y_ref[...]
    mu = jnp.mean(v, axis=-1, keepdims=True)
    d = v - mu
    var = jnp.mean(d * d, axis=-1, keepdims=True)
    out = d * lax.rsqrt(var + LN_EPS) * g_ref[...] + b_ref[...]
    o_ref[...] = out
    ob_ref[...] = out.astype(BF16)


def _residual_ln(x, y, g, b, alpha, tm=256):
    m, d = x.shape
    tm = min(tm, m)
    row = pl.BlockSpec((tm, d), lambda i: (i, 0))
    vec = pl.BlockSpec((1, d), lambda i: (0, 0))
    return pl.pallas_call(
        functools.partial(_ln_kernel, alpha),
        grid=(m // tm,),
        in_specs=[row, row, vec, vec],
        out_specs=[row, row],
        out_shape=[jax.ShapeDtypeStruct((m, d), F32), jax.ShapeDtypeStruct((m, d), BF16)],
        compiler_params=_cparams(("parallel",)),
    )(x, y, g.reshape(1, d), b.reshape(1, d))


def _fgate_kernel(apply_gate, tb, x_ref, bias_ref, c0_ref, lf_ref, c_ref, carry):
    @pl.when(pl.program_id(1) == 0)
    def _():
        carry[...] = c0_ref[0]

    x = x_ref[0]
    lf = _log_sigmoid(x + bias_ref[...]) if apply_gate else x
    c = _dot_exact_lhs(_tri_incl(tb), lf) + carry[...]
    lf_ref[0] = lf
    c_ref[0] = c
    carry[...] = c[tb - 1:tb, :]


def _fgate(x, col_block, bias, c0, apply_gate, tb):
    n, t, _ = x.shape
    tb = min(tb, t)
    blk = pl.BlockSpec((1, tb, LANES), lambda i, j: (i, j, 0))
    return pl.pallas_call(
        functools.partial(_fgate_kernel, apply_gate, tb),
        grid=(n, t // tb),
        in_specs=[pl.BlockSpec((1, tb, LANES), lambda i, j: (i, j, col_block)),
                  pl.BlockSpec((1, LANES), lambda i, j: (0, 0)),
                  pl.BlockSpec((1, 1, LANES), lambda i, j: (i, 0, 0))],
        out_specs=[blk, blk],
        out_shape=[jax.ShapeDtypeStruct((n, t, LANES), F32)] * 2,
        scratch_shapes=[pltpu.VMEM((1, LANES), F32)],
        compiler_params=_cparams(("parallel", "arbitrary")),
    )(x, bias, c0)


def _gdn_kernel(L, qkv_ref, z_ref, gate_ref, buf_ref, convw_ref, alog_ref, dtb_ref, ng_ref, s0_ref,
                o_ref, s_ref, xs):
    @pl.when(pl.program_id(1) == 0)
    def _():
        xs[0:SUBLANES, :] = buf_ref[0]
        s_ref[0] = s0_ref[0]

    xs[SUBLANES:SUBLANES + L, :] = qkv_ref[0]

    def conv_cols(c0):
        acc = xs[5:5 + L, c0:c0 + HEAD_DIM] * convw_ref[0:1, c0:c0 + HEAD_DIM]
        for i in range(1, CONV_W):
            acc = acc + xs[5 + i:5 + i + L, c0:c0 + HEAD_DIM] * convw_ref[i:i + 1, c0:c0 + HEAD_DIM]
        return _silu(acc)

    def l2n(x):
        return x * lax.rsqrt(jnp.sum(x * x, axis=-1, keepdims=True) + RMS_EPS)

    a_in = gate_ref[0, :, 0:LANES]
    b_in = gate_ref[0, :, LANES:2 * LANES]
    g = -jnp.exp(alog_ref[...]) * _softplus(a_in + dtb_ref[...])
    beta = _sigmoid(b_in)
    gc = _dot_exact_lhs(_tri_incl(L), g)
    eye_l = (_iota2((LANES, LANES), 0) == _iota2((LANES, LANES), 1)).astype(BF16)
    g1, g2, g3 = _split3(gc)
    gct = _dot_nt(eye_l, g1) + _dot_nt(eye_l, g2) + _dot_nt(eye_l, g3)

    row = _iota2((L, L), 0)
    col = _iota2((L, L), 1)
    eye = (row == col).astype(F32)
    n_lvl = L.bit_length() - 2

    heads = []
    for h in range(A_HEADS):
        q = l2n(conv_cols(h * HEAD_DIM)) * (HEAD_DIM ** -0.5)
        k = l2n(conv_cols(A_KEY + h * HEAD_DIM))
        v = conv_cols(2 * A_KEY + h * HEAD_DIM)
        gcol = gc[:, h:h + 1]
        bcol = beta[:, h:h + 1]
        glast = gc[L - 1:L, h:h + 1]
        dec = jnp.exp(jnp.where(row >= col, gcol - gct[h:h + 1, :], NEG_BIG))
        eg = jnp.exp(gcol)
        qb = q.astype(BF16)
        kb = k.astype(BF16)
        p = -(bcol * _dot_nt(kb, kb) * jnp.where(row > col, dec, 0.0))
        heads.append(dict(
            qb=qb, p=p, t=eye + p, eg=eg, e_last=jnp.exp(glast),
            qkd=(_dot_nt(qb, kb) * dec).astype(BF16),
            rhs=jnp.concatenate([(bcol * eg) * k, bcol * v], axis=1),
            k_end=(k * jnp.exp(glast - gcol)).astype(BF16),
            s=s_ref[0, h]))

    for hd in heads:
        hd["p"] = _dot3(hd["p"], hd["p"])
    for _ in range(n_lvl - 1):
        for hd in heads:
            pt = _dot3(hd["p"], jnp.concatenate([hd["p"], hd["t"]], axis=1))
            hd["t"] = hd["t"] + pt[:, L:2 * L]
            hd["p"] = pt[:, 0:L]
    for hd in heads:
        hd["t"] = hd["t"] + _dot3(hd["p"], hd["t"])

    for hd in heads:
        tr = _dot3(hd["t"], hd["rhs"])
        sb = hd["s"].astype(BF16)
        u = tr[:, HEAD_DIM:2 * HEAD_DIM] - _dot(tr[:, 0:HEAD_DIM].astype(BF16), sb)
        ub = u.astype(BF16)
        o = hd["eg"] * _dot(hd["qb"], sb) + _dot(hd["qkd"], ub)
        hd["s_new"] = hd["e_last"] * hd["s"] + _dot_tn(hd["k_end"], ub)
        hd["o"] = o * lax.rsqrt(jnp.mean(o * o, axis=-1, keepdims=True) + RMS_EPS) * ng_ref[...]

    for h, hd in enumerate(heads):
        s_ref[0, h] = hd["s_new"]
        zh = z_ref[0, :, h * HEAD_DIM:(h + 1) * HEAD_DIM]
        o_ref[0, :, h * HEAD_DIM:(h + 1) * HEAD_DIM] = (hd["o"] * _silu(zh)).astype(BF16)

    tail = xs[L:L + SUBLANES, :]
    xs[0:SUBLANES, :] = tail


def _gdn(hbig, hgate, conv_buf8, conv_w, a_log, dt_bias, norm_g, s0):
    n, t, _ = hbig.shape
    L = min(t, CHUNK)
    const2 = lambda i, j: (0, 0)
    state = pl.BlockSpec((1, A_HEADS, HEAD_DIM, HEAD_DIM), lambda i, j: (i, 0, 0, 0))
    return pl.pallas_call(
        functools.partial(_gdn_kernel, L),
        grid=(n, t // L),
        in_specs=[pl.BlockSpec((1, L, A_CONV_DIM), lambda i, j: (i, j, COL_QKV_A // A_CONV_DIM)),
                  pl.BlockSpec((1, L, A_VAL), lambda i, j: (i, j, COL_Z_A // A_VAL)),
                  pl.BlockSpec((1, L, 2 * LANES), lambda i, j: (i, j, 0)),
                  pl.BlockSpec((1, SUBLANES, A_CONV_DIM), lambda i, j: (i, 0, 0)),
                  pl.BlockSpec((CONV_W, A_CONV_DIM), const2),
                  pl.BlockSpec((1, LANES), const2),
                  pl.BlockSpec((1, LANES), const2),
                  pl.BlockSpec((1, HEAD_DIM), const2),
                  state],
        out_specs=[pl.BlockSpec((1, L, A_VAL), lambda i, j: (i, j, 0)), state],
        out_shape=[jax.ShapeDtypeStruct((n, t, A_VAL), BF16),
                   jax.ShapeDtypeStruct((n, A_HEADS, HEAD_DIM, HEAD_DIM), F32)],
        scratch_shapes=[pltpu.VMEM((L + SUBLANES, A_CONV_DIM), F32)],
        compiler_params=_cparams(("parallel", "arbitrary")),
    )(hbig, hbig, hgate, conv_buf8, conv_w, a_log, dt_bias, norm_g, s0)


def _gla_kernel(L, q_ref, k_ref, v_ref, r_ref, gate_ref, w2_ref, b2_ref, ng_ref, s0_ref,
                o_ref, s_ref):
    @pl.when(pl.program_id(1) == 0)
    def _():
        s_ref[0] = s0_ref[0]

    z = _dot(gate_ref[0].astype(BF16), w2_ref[...]) + b2_ref[...]
    lg = _log_sigmoid(z) * (1.0 / C_TAU)
    b = _dot_exact_lhs(_tri_incl(L), lg)
    causal = _iota2((L, L), 0) >= _iota2((L, L), 1)

    for h in range(C_HEADS):
        ks = slice(h * C_DK_PAD, (h + 1) * C_DK_PAD)
        vs = slice(h * C_DV, (h + 1) * C_DV)
        bh = b[:, ks]
        blast = bh[L - 1:L, :]
        bref = bh[L // 2:L // 2 + 1, :]
        q = q_ref[0, :, ks] * (C_DK ** -0.5)
        k = k_ref[0, :, ks]
        vb = v_ref[0, :, vs].astype(BF16)
        st = s_ref[0, h]
        o = _dot_nt((q * jnp.exp(bh)).astype(BF16), st.astype(BF16))
        att = _dot_nt((q * jnp.exp(bh - bref)).astype(BF16), (k * jnp.exp(bref - bh)).astype(BF16))
        att = jnp.where(causal, att, 0.0)
        o = o + _dot(att.astype(BF16), vb)
        s_ref[0, h] = st * jnp.exp(blast) + _dot_tn(vb, (k * jnp.exp(blast - bh)).astype(BF16))

        o = o * lax.rsqrt(jnp.mean(o * o, axis=-1, keepdims=True) + RMS_EPS) * ng_ref[...]
        o_ref[0, :, vs] = (o * _silu(r_ref[0, :, vs])).astype(BF16)


def _gla(hbig, hgate, w2p, b2p, norm_g, s0t):
    n, t, _ = hbig.shape
    L = min(t, CHUNK)
    const2 = lambda i, j: (0, 0)
    state = pl.BlockSpec((1, C_HEADS, C_DV, C_DK_PAD), lambda i, j: (i, 0, 0, 0))
    return pl.pallas_call(
        functools.partial(_gla_kernel, L),
        grid=(n, t // L),
        in_specs=[pl.BlockSpec((1, L, C_KEY_PAD), lambda i, j: (i, j, COL_Q_C // C_KEY_PAD)),
                  pl.BlockSpec((1, L, C_KEY_PAD), lambda i, j: (i, j, COL_K_C // C_KEY_PAD)),
                  pl.BlockSpec((1, L, C_VAL), lambda i, j: (i, j, COL_V_C // C_VAL)),
                  pl.BlockSpec((1, L, C_VAL), lambda i, j: (i, j, COL_R_C // C_VAL)),
                  pl.BlockSpec((1, L, LANES), lambda i, j: (i, j, 3)),
                  pl.BlockSpec((LANES, C_KEY_PAD), const2),
                  pl.BlockSpec((1, C_KEY_PAD), const2),
                  pl.BlockSpec((1, C_DV), const2),
                  state],
        out_specs=[pl.BlockSpec((1, L, C_VAL), lambda i, j: (i, j, 0)), state],
        out_shape=[jax.ShapeDtypeStruct((n, t, C_VAL), BF16),
                   jax.ShapeDtypeStruct((n, C_HEADS, C_DV, C_DK_PAD), F32)],
        compiler_params=_cparams(("parallel", "arbitrary")),
    )(hbig, hbig, hbig, hbig, hgate, w2p, b2p, norm_g, s0t)


def _fox_prep_kernel(q_ref, k_ref, v_ref, c_ref, qp_ref, kp_ref, vp_ref):
    tb = q_ref.shape[1]
    lane = _iota2((tb, LANES), 1)
    ones_q = jnp.where((lane >= 3) & (lane < 6), 1.0, 0.0)
    ones_k = jnp.where(lane < 3, 1.0, 0.0)
    c2 = c_ref[0] * LOG2E
    for h in range(B_HEADS):
        hs = slice(h * HEAD_DIM, (h + 1) * HEAD_DIM)
        c1, cm, cl = (t.astype(F32) for t in _split3(c2[:, h:h + 1]))
        ext_q = jnp.where(lane == 0, c1, jnp.where(lane == 1, cm, jnp.where(lane == 2, cl, ones_q)))
        ext_k = jnp.where(lane == 3, -c1, jnp.where(lane == 4, -cm, jnp.where(lane == 5, -cl, ones_k)))
        base = 2 * h * HEAD_DIM
        qp_ref[0, :, base:base + HEAD_DIM] = (q_ref[0, :, hs] * (HEAD_DIM ** -0.5 * LOG2E)).astype(BF16)
        qp_ref[0, :, base + HEAD_DIM:base + 2 * HEAD_DIM] = ext_q.astype(BF16)
        kp_ref[0, :, base:base + HEAD_DIM] = k_ref[0, :, hs].astype(BF16)
        kp_ref[0, :, base + HEAD_DIM:base + 2 * HEAD_DIM] = ext_k.astype(BF16)
    vp_ref[0] = v_ref[0].astype(BF16)


def _fox_prep_cache_kernel(q_ref, k_ref, v_ref, c_ref, *rest):
    qp_ref, kp_ref, vp_ref, kc_ref, vc_ref = rest[-5:]
    _fox_prep_kernel(q_ref, k_ref, v_ref, c_ref, qp_ref, kp_ref, vp_ref)
    kc_ref[0] = k_ref[0]
    vc_ref[0] = v_ref[0]


def _fox_prep(hbig, c, layer, depth, caches, tb=256):
    n, t, _ = hbig.shape
    tb = min(tb, t)
    col = lambda blk: (lambda i, j: (i, j, blk))
    wide = pl.BlockSpec((1, tb, 2 * B_WIDTH), col(0))
    cache_spec = pl.BlockSpec((None, 1, tb, B_WIDTH), lambda i, j: (layer, i, j, 0))
    cache_shape = jax.ShapeDtypeStruct((depth, n, t, B_WIDTH), F32)
    in_specs = [pl.BlockSpec((1, tb, B_WIDTH), col(COL_Q_B // B_WIDTH)),
                pl.BlockSpec((1, tb, B_WIDTH), col(COL_K_B // B_WIDTH)),
                pl.BlockSpec((1, tb, B_WIDTH), col(COL_V_B // B_WIDTH)),
                pl.BlockSpec((1, tb, LANES), col(0))]
    args = [hbig, hbig, hbig, c]
    aliases = {}
    if caches is not None:
        in_specs += [pl.BlockSpec(memory_space=pl.ANY)] * 2
        args += list(caches)
        aliases = {4: 3, 5: 4}
    return pl.pallas_call(
        _fox_prep_cache_kernel,
        grid=(n, t // tb),
        in_specs=in_specs,
        out_specs=[wide, wide, pl.BlockSpec((1, tb, B_WIDTH), col(0)), cache_spec, cache_spec],
        out_shape=[jax.ShapeDtypeStruct((n, t, 2 * B_WIDTH), BF16),
                   jax.ShapeDtypeStruct((n, t, 2 * B_WIDTH), BF16),
                   jax.ShapeDtypeStruct((n, t, B_WIDTH), BF16),
                   cache_shape, cache_shape],
        input_output_aliases=aliases,
        compiler_params=_cparams(("parallel", "parallel")),
    )(*args)


def _fox_prompt_kernel(tq, q_ref, k_ref, v_ref, o_ref, m_sc, l_sc, acc_sc):
    qi = pl.program_id(2)
    q = q_ref[0]
    m_sc[...] = jnp.full(m_sc.shape, NEG_BIG, F32)
    l_sc[...] = jnp.zeros(l_sc.shape, F32)
    acc_sc[...] = jnp.zeros(acc_sc.shape, F32)
    n_tiles = tq // LANES

    def block(j, masked):
        off = pl.multiple_of(j * tq, tq)
        s = _dot_nt(q, k_ref[0, pl.ds(off, tq), :])
        if masked:
            s = jnp.where(_iota2((tq, tq), 1) <= _iota2((tq, tq), 0), s, NEG_BIG)
        tiles = [s[:, c * LANES:(c + 1) * LANES] for c in range(n_tiles)]
        mx = tiles[0]
        for tl in tiles[1:]:
            mx = jnp.maximum(mx, tl)
        m_prev = m_sc[...]
        m_new = jnp.maximum(m_prev, jnp.max(mx, axis=-1, keepdims=True))
        alpha = jnp.exp2(m_prev - m_new)
        ps = [jnp.exp2(tl - m_new) for tl in tiles]
        psum = ps[0]
        for pt in ps[1:]:
            psum = psum + pt
        l_sc[...] = alpha * l_sc[...] + psum
        p = jnp.concatenate([pt.astype(BF16) for pt in ps], axis=1)
        acc_sc[...] = alpha * acc_sc[...] + _dot(p, v_ref[0, pl.ds(off, tq), :])
        m_sc[...] = m_new

    def body(j, carry):
        block(j, False)
        return carry

    lax.fori_loop(0, qi, body, 0)
    block(qi, True)
    o_ref[0] = (acc_sc[...] / jnp.sum(l_sc[...], axis=-1, keepdims=True)).astype(BF16)


def _fox_prompt(qp, kp, vp, tq=512):
    n, t, _ = vp.shape
    tq = min(tq, t)
    return pl.pallas_call(
        functools.partial(_fox_prompt_kernel, tq),
        grid=(n, B_HEADS, t // tq),
        in_specs=[pl.BlockSpec((1, tq, 2 * HEAD_DIM), lambda i, h, qi: (i, qi, h)),
                  pl.BlockSpec((1, t, 2 * HEAD_DIM), lambda i, h, qi: (i, 0, h)),
                  pl.BlockSpec((1, t, HEAD_DIM), lambda i, h, qi: (i, 0, h))],
        out_specs=pl.BlockSpec((1, tq, HEAD_DIM), lambda i, h, qi: (i, qi, h)),
        out_shape=jax.ShapeDtypeStruct((n, t, B_WIDTH), BF16),
        scratch_shapes=[pltpu.VMEM((tq, LANES), F32),
                        pltpu.VMEM((tq, LANES), F32),
                        pltpu.VMEM((tq, HEAD_DIM), F32)],
        compiler_params=_cparams(("parallel", "parallel", "arbitrary")),
    )(qp, kp, vp)


def _fox_sample_kernel(t, q_ref, kn_ref, vn_ref, kp_ref, vp_ref, cq_ref, ckp_ref, ckn_ref, o_ref):
    qb = (q_ref[0] * (HEAD_DIM ** -0.5)).astype(BF16)
    cq = cq_ref[0, 0]
    sp = _dot_nt(qb, kp_ref[0].astype(BF16)) + (cq - ckp_ref[0, 0])
    sn = _dot_nt(qb, kn_ref[0].astype(BF16)) + (cq - ckn_ref[0, 0])
    sn = jnp.where(_iota2((t, t), 1) <= _iota2((t, t), 0), sn, NEG_BIG)
    m = jnp.maximum(jnp.max(sp, axis=-1, keepdims=True), jnp.max(sn, axis=-1, keepdims=True))
    pp = jnp.exp(sp - m)
    pn = jnp.exp(sn - m)
    den = jnp.sum(pp, axis=-1, keepdims=True) + jnp.sum(pn, axis=-1, keepdims=True)
    o = _dot(pp.astype(BF16), vp_ref[0].astype(BF16)) + _dot(pn.astype(BF16), vn_ref[0].astype(BF16))
    o_ref[0] = (o / den).astype(BF16)


def _fox_sample(hbig, k_past, v_past, layer, cq, ckp, ckn):
    n, t, _ = hbig.shape
    p = k_past.shape[2]
    head_col = lambda base: (lambda i, h: (i, 0, base // HEAD_DIM + h))
    past = pl.BlockSpec((None, 1, p, HEAD_DIM), lambda i, h: (layer, i, 0, h))
    return pl.pallas_call(
        functools.partial(_fox_sample_kernel, t),
        grid=(n, B_HEADS),
        in_specs=[pl.BlockSpec((1, t, HEAD_DIM), head_col(COL_Q_B)),
                  pl.BlockSpec((1, t, HEAD_DIM), head_col(COL_K_B)),
                  pl.BlockSpec((1, t, HEAD_DIM), head_col(COL_V_B)),
                  past, past,
                  pl.BlockSpec((1, 1, t, 1), lambda i, h: (i, h, 0, 0)),
                  pl.BlockSpec((1, 1, 1, p), lambda i, h: (i, h, 0, 0)),
                  pl.BlockSpec((1, 1, 1, t), lambda i, h: (i, h, 0, 0))],
        out_specs=pl.BlockSpec((1, t, HEAD_DIM), lambda i, h: (i, 0, h)),
        out_shape=jax.ShapeDtypeStruct((n, t, B_WIDTH), BF16),
        compiler_params=_cparams(("parallel", "parallel")),
    )(hbig, hbig, hbig, k_past, v_past, cq, ckp, ckn)


def _pad_cols(w, width):
    return jnp.pad(w, [(0, 0)] * (w.ndim - 1) + [(0, width - w.shape[-1])])


def _pad_heads_c(w):
    lead = w.shape[:-1]
    w = w.reshape(lead + (C_HEADS, C_DK))
    w = jnp.pad(w, [(0, 0)] * len(lead) + [(0, 0), (0, C_DK_PAD - C_DK)])
    return w.reshape(lead + (C_KEY_PAD,))


def _prep_w_in(w):
    o = 0
    parts = {}
    for name, size in (("qkv_a", A_CONV_DIM), ("z_a", A_VAL), ("a_a", A_HEADS), ("b_a", A_HEADS),
                       ("q_b", B_WIDTH), ("k_b", B_WIDTH), ("v_b", B_WIDTH), ("f_b", B_HEADS),
                       ("q_c", C_KEY), ("k_c", C_KEY), ("v_c", C_VAL), ("r_c", C_VAL), ("lr_c", C_RANK)):
        parts[name] = w[..., o:o + size].astype(BF16)
        o += size
    big = jnp.concatenate([parts["qkv_a"], parts["z_a"], parts["v_c"], parts["r_c"],
                           parts["q_b"], parts["k_b"], parts["v_b"],
                           _pad_heads_c(parts["q_c"]), _pad_heads_c(parts["k_c"])], axis=-1)
    gate = jnp.concatenate([_pad_cols(parts[nm], LANES) for nm in ("a_a", "b_a", "f_b", "lr_c")], axis=-1)
    return big, gate


def _pad_vec(v):
    return _pad_cols(v.reshape(1, -1).astype(F32), LANES)


def _mixer(xb, n, t, conv_buf8, s_a0, past_b, s_c0t, w, layer, depth, kv_caches):
    m = n * t
    hbig = _matmul(xb, w["w_big"], layer, 1024, 1024, F32).reshape(n, t, IN_BIG)
    hgate = _matmul(xb, w["w_gate4"], layer, 1024, IN_GATE, F32).reshape(n, t, IN_GATE)

    oa, s_a = _gdn(hbig, hgate, conv_buf8, w["conv_w"], w["a_log"], w["dt_bias"], w["a_norm_g"], s_a0)
    oc, s_ct = _gla(hbig, hgate, w["c_w2p"], w["c_b2p"], w["c_norm_g"], s_c0t)

    zero_c = jnp.zeros((n, 1, LANES), F32)
    if past_b is None:
        lf, c = _fgate(hgate, 2, w["f_bias"], zero_c, True, 256)
        qp, kp, vp, k_cache, v_cache = _fox_prep(hbig, c, layer, depth, kv_caches)
        kv_caches = (k_cache, v_cache)
        ob = _fox_prompt(qp, kp, vp)
        kb = vb = None
    else:
        k_past, v_past, lf_past = past_b
        p = k_past.shape[2]
        lf_past = jnp.pad(lf_past.astype(F32), ((0, 0), (0, 0), (0, LANES - B_HEADS)))
        _, c_past = _fgate(lf_past, 0, w["f_bias"], zero_c, False, 256)
        lf, c = _fgate(hgate, 2, w["f_bias"], c_past[:, p - 1:p, :], True, t)
        cq = jnp.swapaxes(c[:, :, :B_HEADS], 1, 2)[..., None]
        ckn = jnp.swapaxes(c[:, :, :B_HEADS], 1, 2)[:, :, None, :]
        ckp = jnp.swapaxes(c_past[:, :, :B_HEADS], 1, 2)[:, :, None, :]
        ob = _fox_sample(hbig, k_past, v_past, layer, cq, ckp, ckn)

    y = _outproj(oa.reshape(m, A_VAL), ob.reshape(m, B_WIDTH), oc.reshape(m, C_VAL),
                 w["w_out"], layer, 1024, 1024)

    conv_new = hbig[:, t - (CONV_W - 1):, COL_QKV_A:COL_QKV_A + A_CONV_DIM]
    if past_b is not None:
        kb = hbig[:, :, COL_K_B:COL_K_B + B_WIDTH].reshape(n, t, B_HEADS, HEAD_DIM)
        vb = hbig[:, :, COL_V_B:COL_V_B + B_WIDTH].reshape(n, t, B_HEADS, HEAD_DIM)
    s_c = jnp.swapaxes(s_ct, 2, 3)[:, :, :C_DK, :]
    return y, (conv_new, s_a, kb, vb, lf[:, :, :B_HEADS], s_c), kv_caches


def _layer(x, xb, n, t, conv_buf8, s_a0, past_b, s_c0t, w, layer, depth, alpha, kv_caches=None):
    y, st, kv_caches = _mixer(xb, n, t, conv_buf8, s_a0, past_b, s_c0t, w, layer, depth, kv_caches)
    x, xb = _residual_ln(x, y, w["ln1_g"], w["ln1_b"], alpha)
    hmid = _swiglu(xb, w["w_gate"], w["w_up"], layer, 1024, 256)
    y2 = _matmul(hmid, w["w_down"], layer, 512, 512, F32)
    x, xb = _residual_ln(x, y2, w["ln2_g"], w["ln2_b"], alpha)
    return x, xb, st, kv_caches


def kernel(x_prompt, x_sample, state_a_conv, state_a_rec, cache_b_k, cache_b_v, cache_b_logf, state_c_rec, w_in, conv_w, a_log, dt_bias, a_norm_g, f_bias, c_w2, c_b2, c_norm_g, w_out, ln1_g, ln1_b, w_gate, w_up, w_down, ln2_g, ln2_b):
    depth = w_in.shape[0]
    alpha = (2 * depth) ** DEPTH_ALPHA_POW
    nb, tp, _ = x_prompt.shape
    ns, ts, _ = x_sample.shape

    hp = x_prompt.reshape(nb * tp, D_MODEL).astype(F32)
    hs = x_sample.reshape(ns * ts, D_MODEL).astype(F32)
    hpb = hp.astype(BF16)
    hsb = hs.astype(BF16)
    p_states, s_states = [], []
    w_big, w_gate4 = _prep_w_in(w_in)
    w_out_b, w_gate_b, w_up_b, w_down_b = (a.astype(BF16) for a in (w_out, w_gate, w_up, w_down))
    past_len = cache_b_k.shape[2]
    cache_k = cache_b_k.reshape(depth, ns, past_len, B_WIDTH).astype(F32)
    cache_v = cache_b_v.reshape(depth, ns, past_len, B_WIDTH).astype(F32)
    prompt_kv = None
    for l in range(depth):
        w = {
            "w_big": w_big, "w_gate4": w_gate4,
            "conv_w": conv_w[l].astype(F32),
            "a_log": _pad_vec(a_log[l]), "dt_bias": _pad_vec(dt_bias[l]),
            "a_norm_g": a_norm_g[l].reshape(1, HEAD_DIM).astype(F32),
            "f_bias": _pad_vec(f_bias[l]),
            "c_w2p": jnp.pad(_pad_heads_c(c_w2[l]), ((0, LANES - C_RANK), (0, 0))).astype(BF16),
            "c_b2p": _pad_heads_c(c_b2[l].reshape(1, C_KEY)).astype(F32),
            "c_norm_g": c_norm_g[l].reshape(1, C_DV).astype(F32),
            "w_out": w_out_b,
            "ln1_g": ln1_g[l], "ln1_b": ln1_b[l], "ln2_g": ln2_g[l], "ln2_b": ln2_b[l],
            "w_gate": w_gate_b, "w_up": w_up_b, "w_down": w_down_b,
        }
        hp, hpb, stp, prompt_kv = _layer(
            hp, hpb, nb, tp,
            jnp.zeros((nb, SUBLANES, A_CONV_DIM), F32),
            jnp.zeros((nb, A_HEADS, HEAD_DIM, HEAD_DIM), F32),
            None,
            jnp.zeros((nb, C_HEADS, C_DV, C_DK_PAD), F32), w, l, depth, alpha, prompt_kv)
        p_states.append(stp)
        buf8 = jnp.pad(state_a_conv[l].astype(F32), ((0, 0), (SUBLANES - (CONV_W - 1), 0), (0, 0)))
        s_c0t = jnp.pad(jnp.swapaxes(state_c_rec[l].astype(F32), 2, 3),
                        ((0, 0), (0, 0), (0, 0), (0, C_DK_PAD - C_DK)))
        hs, hsb, sts, _ = _layer(
            hs, hsb, ns, ts, buf8, state_a_rec[l].astype(F32),
            (cache_k, cache_v, cache_b_logf[l]), s_c0t, w, l, depth, alpha)
        s_states.append(sts)

    dp, ds = x_prompt.dtype, x_sample.dtype
    stack = lambda states, i, dt: jnp.stack([s[i] for s in states], axis=0).astype(dt)
    prompt_k, prompt_v = (a.reshape(depth, nb, tp, B_HEADS, HEAD_DIM).astype(dp) for a in prompt_kv)
    return ((hp.reshape(nb, tp, D_MODEL).astype(dp), hs.reshape(ns, ts, D_MODEL).astype(ds))
            + (stack(p_states, 0, dp), stack(p_states, 1, dp), prompt_k, prompt_v,
               stack(p_states, 4, dp), stack(p_states, 5, dp))
            + tuple(stack(s_states, i, ds) for i in range(6)))
```

```python
import functools

import jax
import jax.numpy as jnp
from jax import lax
from jax.experimental import pallas as pl
from jax.experimental.pallas import tpu as pltpu

F32 = jnp.float32
BF16 = jnp.bfloat16

D_MODEL = 4096
CHUNK = 64
HEAD_DIM = 128
A_HEADS = 12
A_KEY = A_HEADS * HEAD_DIM
A_VAL = A_HEADS * HEAD_DIM
A_CONV_DIM = 2 * A_KEY + A_VAL
CONV_W = 4
B_HEADS = 8
B_WIDTH = B_HEADS * HEAD_DIM
C_HEADS = 4
C_DV = 384
C_DK = 192
C_DK_PAD = 256
C_KEY = C_HEADS * C_DK
C_KEY_PAD = C_HEADS * C_DK_PAD
C_VAL = C_HEADS * C_DV
C_RANK = 16
C_TAU = 16.0
D_FF = 11008
DEPTH_ALPHA_POW = 0.25
LN_EPS = 1e-5
RMS_EPS = 1e-6

LANES = 128
SUBLANES = 8
VMEM_LIMIT = 56 * 1024 * 1024

COL_QKV_A = 0
COL_Z_A = 4608
COL_V_C = 6144
COL_R_C = 7680
COL_Q_B = 9216
COL_K_B = 10240
COL_V_B = 11264
COL_Q_C = 12288
COL_K_C = 13312
IN_BIG = 14336
GATE_BLOCKS = 4
IN_GATE = GATE_BLOCKS * LANES

NEG_BIG = -1e30
LOG2E = 1.4426950408889634
FOX_HEADS_PER_STEP = 2
GDN_HEADS_PER_GROUP = 3


def _cparams(sem):
    return pltpu.CompilerParams(dimension_semantics=sem, vmem_limit_bytes=VMEM_LIMIT)


def _sigmoid(x):
    return 1.0 / (1.0 + jnp.exp(-x))


def _silu(x):
    return x * _sigmoid(x)


def _softplus(x):
    return jnp.maximum(x, 0.0) + jnp.log(1.0 + jnp.exp(-jnp.abs(x)))


def _log_sigmoid(x):
    return -_softplus(-x)


def _split3(x):
    x1 = x.astype(BF16)
    r1 = x - x1.astype(F32)
    x2 = r1.astype(BF16)
    x3 = (r1 - x2.astype(F32)).astype(BF16)
    return x1, x2, x3


def _dot(a, b):
    return jnp.dot(a, b, preferred_element_type=F32)


def _dot_nt(a, b):
    return lax.dot_general(a, b, (((1,), (1,)), ((), ())), preferred_element_type=F32)


def _dot_tn(a, b):
    return lax.dot_general(a, b, (((0,), (0,)), ((), ())), preferred_element_type=F32)


def _dot_exact_lhs(a_bf16, x):
    x1, x2, x3 = _split3(x)
    return _dot(a_bf16, x1) + _dot(a_bf16, x2) + _dot(a_bf16, x3)


def _iota2(shape, dim):
    return lax.broadcasted_iota(jnp.int32, shape, dim)


def _tri_incl(n):
    return (_iota2((n, n), 1) <= _iota2((n, n), 0)).astype(BF16)


def _mm_kernel(x_ref, w_ref, o_ref):
    o_ref[...] = _dot(x_ref[...], w_ref[...]).astype(o_ref.dtype)


def _matmul(x, w, layer, tm, tn, out_dtype):
    m, k = x.shape
    n = w.shape[2]
    tm = min(tm, m)
    return pl.pallas_call(
        _mm_kernel,
        grid=(m // tm, n // tn),
        in_specs=[pl.BlockSpec((tm, k), lambda i, j: (i, 0)),
                  pl.BlockSpec((None, k, tn), lambda i, j: (layer, 0, j))],
        out_specs=pl.BlockSpec((tm, tn), lambda i, j: (i, j)),
        out_shape=jax.ShapeDtypeStruct((m, n), out_dtype),
        compiler_params=_cparams(("parallel", "parallel")),
    )(x, w)


def _outproj_kernel(a_ref, b_ref, c_ref, w_ref, o_ref):
    o_ref[...] = (_dot(a_ref[...], w_ref[0:A_VAL, :])
                  + _dot(b_ref[...], w_ref[A_VAL:A_VAL + B_WIDTH, :])
                  + _dot(c_ref[...], w_ref[A_VAL + B_WIDTH:D_MODEL, :]))


def _outproj(oa, ob, oc, w, layer, tm, tn):
    m = oa.shape[0]
    tm = min(tm, m)
    rows = lambda width: pl.BlockSpec((tm, width), lambda i, j: (i, 0))
    return pl.pallas_call(
        _outproj_kernel,
        grid=(m // tm, D_MODEL // tn),
        in_specs=[rows(A_VAL), rows(B_WIDTH), rows(C_VAL),
                  pl.BlockSpec((None, D_MODEL, tn), lambda i, j: (layer, 0, j))],
        out_specs=pl.BlockSpec((tm, tn), lambda i, j: (i, j)),
        out_shape=jax.ShapeDtypeStruct((m, D_MODEL), F32),
        compiler_params=_cparams(("parallel", "parallel")),
    )(oa, ob, oc, w)


def _swiglu_kernel(x_ref, wg_ref, wu_ref, o_ref):
    x = x_ref[...]
    a = _dot(x, wg_ref[...])
    b = _dot(x, wu_ref[...])
    o_ref[...] = (_silu(a) * b).astype(o_ref.dtype)


def _swiglu(x, wg, wu, layer, tm, tn):
    m, k = x.shape
    n = wg.shape[2]
    tm = min(tm, m)
    wspec = pl.BlockSpec((None, k, tn), lambda i, j: (layer, 0, j))
    return pl.pallas_call(
        _swiglu_kernel,
        grid=(m // tm, n // tn),
        in_specs=[pl.BlockSpec((tm, k), lambda i, j: (i, 0)), wspec, wspec],
        out_specs=pl.BlockSpec((tm, tn), lambda i, j: (i, j)),
        out_shape=jax.ShapeDtypeStruct((m, n), BF16),
        compiler_params=_cparams(("parallel", "parallel")),
    )(x, wg, wu)


def _ln_kernel(alpha, x_ref, y_ref, g_ref, b_ref, o_ref, ob_ref):
    v = alpha * x_ref[...] + y_ref[...]
    mu = jnp.mean(v, axis=-1, keepdims=True)
    d = v - mu
    var = jnp.mean(d * d, axis=-1, keepdims=True)
    out = d * lax.rsqrt(var + LN_EPS) * g_ref[...] + b_ref[...]
    o_ref[...] = out
    ob_ref[...] = out.astype(BF16)


def _residual_ln(x, y, g, b, alpha, tm=256):
    m, d = x.shape
    tm = min(tm, m)
    row = pl.BlockSpec((tm, d), lambda i: (i, 0))
    vec = pl.BlockSpec((1, d), lambda i: (0, 0))
    return pl.pallas_call(
        functools.partial(_ln_kernel, alpha),
        grid=(m // tm,),
        in_specs=[row, row, vec, vec],
        out_specs=[row, row],
        out_shape=[jax.ShapeDtypeStruct((m, d), F32), jax.ShapeDtypeStruct((m, d), BF16)],
        compiler_params=_cparams(("parallel",)),
    )(x, y, g.reshape(1, d), b.reshape(1, d))


def _fgate_kernel(apply_gate, tb, x_ref, bias_ref, c0_ref, lf_ref, c_ref, carry):
    @pl.when(pl.program_id(1) == 0)
    def _():
        carry[...] = c0_ref[0]

    x = x_ref[0]
    lf = _log_sigmoid(x + bias_ref[...]) if apply_gate else x
    c = _dot_exact_lhs(_tri_incl(tb), lf) + carry[...]
    lf_ref[0] = lf
    c_ref[0] = c
    carry[...] = c[tb - 1:tb, :]


def _fgate(x, col_block, bias, c0, apply_gate, tb):
    n, t, _ = x.shape
    tb = min(tb, t)
    blk = pl.BlockSpec((1, tb, LANES), lambda i, j: (i, j, 0))
    return pl.pallas_call(
        functools.partial(_fgate_kernel, apply_gate, tb),
        grid=(n, t // tb),
        in_specs=[pl.BlockSpec((1, tb, LANES), lambda i, j: (i, j, col_block)),
                  pl.BlockSpec((1, LANES), lambda i, j: (0, 0)),
                  pl.BlockSpec((1, 1, LANES), lambda i, j: (i, 0, 0))],
        out_specs=[blk, blk],
        out_shape=[jax.ShapeDtypeStruct((n, t, LANES), F32)] * 2,
        scratch_shapes=[pltpu.VMEM((1, LANES), F32)],
        compiler_params=_cparams(("parallel", "arbitrary")),
    )(x, bias, c0)


def _gdn_kernel(L, qkv_ref, z_ref, gate_ref, buf_ref, convw_ref, alog_ref, dtb_ref, ng_ref, s0_ref,
                o_ref, s_ref, xs):
    @pl.when(pl.program_id(1) == 0)
    def _():
        xs[0:SUBLANES, :] = buf_ref[0]
        s_ref[0] = s0_ref[0]

    xs[SUBLANES:SUBLANES + L, :] = qkv_ref[0]

    def conv_cols(c0):
        acc = xs[5:5 + L, c0:c0 + HEAD_DIM] * convw_ref[0:1, c0:c0 + HEAD_DIM]
        for i in range(1, CONV_W):
            acc = acc + xs[5 + i:5 + i + L, c0:c0 + HEAD_DIM] * convw_ref[i:i + 1, c0:c0 + HEAD_DIM]
        return _silu(acc)

    def l2n(x):
        return x * lax.rsqrt(jnp.sum(x * x, axis=-1, keepdims=True) + RMS_EPS)

    a_in = gate_ref[0, :, 0:LANES]
    b_in = gate_ref[0, :, LANES:2 * LANES]
    g = -jnp.exp(alog_ref[...]) * _softplus(a_in + dtb_ref[...])
    beta = _sigmoid(b_in)
    gc = _dot_exact_lhs(_tri_incl(L), g)
    eye_l = (_iota2((LANES, LANES), 0) == _iota2((LANES, LANES), 1)).astype(BF16)
    g1, g2, g3 = _split3(gc)
    gct = _dot_nt(eye_l, g1) + _dot_nt(eye_l, g2) + _dot_nt(eye_l, g3)

    row = _iota2((L, L), 0)
    col = _iota2((L, L), 1)
    eye = (row == col).astype(F32)
    n_lvl = L.bit_length() - 2

    def mm(a, b):
        return _dot(a.astype(BF16), b.astype(BF16))

    def st_prep(hd):
        h = hd["h"]
        q = l2n(conv_cols(h * HEAD_DIM)) * (HEAD_DIM ** -0.5)
        k = l2n(conv_cols(A_KEY + h * HEAD_DIM))
        v = conv_cols(2 * A_KEY + h * HEAD_DIM)
        gcol = gc[:, h:h + 1]
        bcol = beta[:, h:h + 1]
        glast = gc[L - 1:L, h:h + 1]
        dec = jnp.exp(jnp.where(row >= col, gcol - gct[h:h + 1, :], NEG_BIG))
        eg = jnp.exp(gcol)
        qb = q.astype(BF16)
        kb = k.astype(BF16)
        p = -(bcol * _dot_nt(kb, kb) * jnp.where(row > col, dec, 0.0))
        hd.update(qb=qb, p=p, t=eye + p, eg=eg, e_last=jnp.exp(glast),
                  qkd=(_dot_nt(qb, kb) * dec).astype(BF16),
                  rhs=jnp.concatenate([(bcol * eg) * k, bcol * v], axis=1),
                  k_end=(k * jnp.exp(glast - gcol)).astype(BF16),
                  s=s_ref[0, h])

    def st_square(hd):
        hd["p"] = mm(hd["p"], hd["p"])

    def st_level(hd):
        pt = mm(hd["p"], jnp.concatenate([hd["p"], hd["t"]], axis=1))
        hd["t"] = hd["t"] + pt[:, L:2 * L]
        hd["p"] = pt[:, 0:L]

    def st_last_level(hd):
        hd["t"] = hd["t"] + mm(hd["p"], hd["t"])

    def st_solve(hd):
        hd["tr"] = mm(hd["t"], hd["rhs"])

    def st_state(hd):
        tr = hd["tr"]
        sb = hd["s"].astype(BF16)
        hd["ub"] = (tr[:, HEAD_DIM:2 * HEAD_DIM] - _dot(tr[:, 0:HEAD_DIM].astype(BF16), sb)).astype(BF16)
        hd["qs"] = _dot(hd["qb"], sb)

    def st_out(hd):
        o = hd["eg"] * hd["qs"] + _dot(hd["qkd"], hd["ub"])
        hd["s_new"] = hd["e_last"] * hd["s"] + _dot_tn(hd["k_end"], hd["ub"])
        hd["o"] = o * lax.rsqrt(jnp.mean(o * o, axis=-1, keepdims=True) + RMS_EPS) * ng_ref[...]

    stages = [st_prep, st_square] + [st_level] * (n_lvl - 1) + [st_last_level, st_solve, st_state, st_out]
    heads = [dict(h=h) for h in range(A_HEADS)]
    groups = [heads[i:i + GDN_HEADS_PER_GROUP] for i in range(0, A_HEADS, GDN_HEADS_PER_GROUP)]
    for wave in range(len(groups) + len(stages) - 1):
        for g, group in enumerate(groups):
            if 0 <= wave - g < len(stages):
                for hd in group:
                    stages[wave - g](hd)

    for h, hd in enumerate(heads):
        s_ref[0, h] = hd["s_new"]
        zh = z_ref[0, :, h * HEAD_DIM:(h + 1) * HEAD_DIM]
        o_ref[0, :, h * HEAD_DIM:(h + 1) * HEAD_DIM] = (hd["o"] * _silu(zh)).astype(BF16)

    tail = xs[L:L + SUBLANES, :]
    xs[0:SUBLANES, :] = tail


def _gdn(hbig, hgate, conv_buf8, conv_w, a_log, dt_bias, norm_g, s0):
    n, t, _ = hbig.shape
    L = min(t, CHUNK)
    const2 = lambda i, j: (0, 0)
    state = pl.BlockSpec((1, A_HEADS, HEAD_DIM, HEAD_DIM), lambda i, j: (i, 0, 0, 0))
    return pl.pallas_call(
        functools.partial(_gdn_kernel, L),
        grid=(n, t // L),
        in_specs=[pl.BlockSpec((1, L, A_CONV_DIM), lambda i, j: (i, j, COL_QKV_A // A_CONV_DIM)),
                  pl.BlockSpec((1, L, A_VAL), lambda i, j: (i, j, COL_Z_A // A_VAL)),
                  pl.BlockSpec((1, L, 2 * LANES), lambda i, j: (i, j, 0)),
                  pl.BlockSpec((1, SUBLANES, A_CONV_DIM), lambda i, j: (i, 0, 0)),
                  pl.BlockSpec((CONV_W, A_CONV_DIM), const2),
                  pl.BlockSpec((1, LANES), const2),
                  pl.BlockSpec((1, LANES), const2),
                  pl.BlockSpec((1, HEAD_DIM), const2),
                  state],
        out_specs=[pl.BlockSpec((1, L, A_VAL), lambda i, j: (i, j, 0)), state],
        out_shape=[jax.ShapeDtypeStruct((n, t, A_VAL), BF16),
                   jax.ShapeDtypeStruct((n, A_HEADS, HEAD_DIM, HEAD_DIM), F32)],
        scratch_shapes=[pltpu.VMEM((L + SUBLANES, A_CONV_DIM), F32)],
        compiler_params=_cparams(("parallel", "arbitrary")),
    )(hbig, hbig, hgate, conv_buf8, conv_w, a_log, dt_bias, norm_g, s0)


def _gla_kernel(L, q_ref, k_ref, v_ref, r_ref, gate_ref, w2_ref, b2_ref, ng_ref, s0_ref,
                o_ref, s_ref):
    @pl.when(pl.program_id(1) == 0)
    def _():
        s_ref[0] = s0_ref[0]

    z = _dot(gate_ref[0].astype(BF16), w2_ref[...]) + b2_ref[...]
    lg = _log_sigmoid(z) * (1.0 / C_TAU)
    b = _dot_exact_lhs(_tri_incl(L), lg)
    causal = _iota2((L, L), 0) >= _iota2((L, L), 1)

    heads = []
    for h in range(C_HEADS):
        ks = slice(h * C_DK_PAD, (h + 1) * C_DK_PAD)
        vs = slice(h * C_DV, (h + 1) * C_DV)
        bh = b[:, ks]
        blast = bh[L - 1:L, :]
        bref = bh[L // 2:L // 2 + 1, :]
        q = q_ref[0, :, ks] * (C_DK ** -0.5)
        k = k_ref[0, :, ks]
        heads.append(dict(
            vs=vs, vb=v_ref[0, :, vs].astype(BF16), st=s_ref[0, h],
            q_state=(q * jnp.exp(bh)).astype(BF16),
            q_in=(q * jnp.exp(bh - bref)).astype(BF16),
            k_in=(k * jnp.exp(bref - bh)).astype(BF16),
            k_end=(k * jnp.exp(blast - bh)).astype(BF16),
            e_last=jnp.exp(blast)))
    for hd in heads:
        hd["att"] = jnp.where(causal, _dot_nt(hd["q_in"], hd["k_in"]), 0.0).astype(BF16)
        hd["o"] = _dot_nt(hd["q_state"], hd["st"].astype(BF16))
        hd["st_new"] = hd["st"] * hd["e_last"] + _dot_tn(hd["vb"], hd["k_end"])
    for hd in heads:
        o = hd["o"] + _dot(hd["att"], hd["vb"])
        hd["o"] = o * lax.rsqrt(jnp.mean(o * o, axis=-1, keepdims=True) + RMS_EPS) * ng_ref[...]
    for h, hd in enumerate(heads):
        s_ref[0, h] = hd["st_new"]
        o_ref[0, :, hd["vs"]] = (hd["o"] * _silu(r_ref[0, :, hd["vs"]])).astype(BF16)


def _gla(hbig, hgate, w2p, b2p, norm_g, s0t):
    n, t, _ = hbig.shape
    L = min(t, CHUNK)
    const2 = lambda i, j: (0, 0)
    state = pl.BlockSpec((1, C_HEADS, C_DV, C_DK_PAD), lambda i, j: (i, 0, 0, 0))
    return pl.pallas_call(
        functools.partial(_gla_kernel, L),
        grid=(n, t // L),
        in_specs=[pl.BlockSpec((1, L, C_KEY_PAD), lambda i, j: (i, j, COL_Q_C // C_KEY_PAD)),
                  pl.BlockSpec((1, L, C_KEY_PAD), lambda i, j: (i, j, COL_K_C // C_KEY_PAD)),
                  pl.BlockSpec((1, L, C_VAL), lambda i, j: (i, j, COL_V_C // C_VAL)),
                  pl.BlockSpec((1, L, C_VAL), lambda i, j: (i, j, COL_R_C // C_VAL)),
                  pl.BlockSpec((1, L, LANES), lambda i, j: (i, j, 3)),
                  pl.BlockSpec((LANES, C_KEY_PAD), const2),
                  pl.BlockSpec((1, C_KEY_PAD), const2),
                  pl.BlockSpec((1, C_DV), const2),
                  state],
        out_specs=[pl.BlockSpec((1, L, C_VAL), lambda i, j: (i, j, 0)), state],
        out_shape=[jax.ShapeDtypeStruct((n, t, C_VAL), BF16),
                   jax.ShapeDtypeStruct((n, C_HEADS, C_DV, C_DK_PAD), F32)],
        compiler_params=_cparams(("parallel", "arbitrary")),
    )(hbig, hbig, hbig, hbig, hgate, w2p, b2p, norm_g, s0t)


def _fox_prep_kernel(q_ref, k_ref, v_ref, c_ref, qp_ref, kp_ref, vp_ref):
    tb = q_ref.shape[1]
    lane = _iota2((tb, LANES), 1)
    ones_q = jnp.where((lane >= 3) & (lane < 6), 1.0, 0.0)
    ones_k = jnp.where(lane < 3, 1.0, 0.0)
    c2 = c_ref[0] * LOG2E
    for h in range(B_HEADS):
        hs = slice(h * HEAD_DIM, (h + 1) * HEAD_DIM)
        c1, cm, cl = (t.astype(F32) for t in _split3(c2[:, h:h + 1]))
        ext_q = jnp.where(lane == 0, c1, jnp.where(lane == 1, cm, jnp.where(lane == 2, cl, ones_q)))
        ext_k = jnp.where(lane == 3, -c1, jnp.where(lane == 4, -cm, jnp.where(lane == 5, -cl, ones_k)))
        base = 2 * h * HEAD_DIM
        qp_ref[0, :, base:base + HEAD_DIM] = (q_ref[0, :, hs] * (HEAD_DIM ** -0.5 * LOG2E)).astype(BF16)
        qp_ref[0, :, base + HEAD_DIM:base + 2 * HEAD_DIM] = ext_q.astype(BF16)
        kp_ref[0, :, base:base + HEAD_DIM] = k_ref[0, :, hs].astype(BF16)
        kp_ref[0, :, base + HEAD_DIM:base + 2 * HEAD_DIM] = ext_k.astype(BF16)
    vp_ref[0] = v_ref[0].astype(BF16)


def _fox_prep_cache_kernel(q_ref, k_ref, v_ref, c_ref, *rest):
    qp_ref, kp_ref, vp_ref, kc_ref, vc_ref = rest[-5:]
    _fox_prep_kernel(q_ref, k_ref, v_ref, c_ref, qp_ref, kp_ref, vp_ref)
    kc_ref[0] = k_ref[0]
    vc_ref[0] = v_ref[0]


def _fox_prep(hbig, c, layer, depth, caches, tb=256):
    n, t, _ = hbig.shape
    tb = min(tb, t)
    col = lambda blk: (lambda i, j: (i, j, blk))
    wide = pl.BlockSpec((1, tb, 2 * B_WIDTH), col(0))
    cache_spec = pl.BlockSpec((None, 1, tb, B_WIDTH), lambda i, j: (layer, i, j, 0))
    cache_shape = jax.ShapeDtypeStruct((depth, n, t, B_WIDTH), F32)
    in_specs = [pl.BlockSpec((1, tb, B_WIDTH), col(COL_Q_B // B_WIDTH)),
                pl.BlockSpec((1, tb, B_WIDTH), col(COL_K_B // B_WIDTH)),
                pl.BlockSpec((1, tb, B_WIDTH), col(COL_V_B // B_WIDTH)),
                pl.BlockSpec((1, tb, LANES), col(0))]
    args = [hbig, hbig, hbig, c]
    aliases = {}
    if caches is not None:
        in_specs += [pl.BlockSpec(memory_space=pl.ANY)] * 2
        args += list(caches)
        aliases = {4: 3, 5: 4}
    return pl.pallas_call(
        _fox_prep_cache_kernel,
        grid=(n, t // tb),
        in_specs=in_specs,
        out_specs=[wide, wide, pl.BlockSpec((1, tb, B_WIDTH), col(0)), cache_spec, cache_spec],
        out_shape=[jax.ShapeDtypeStruct((n, t, 2 * B_WIDTH), BF16),
                   jax.ShapeDtypeStruct((n, t, 2 * B_WIDTH), BF16),
                   jax.ShapeDtypeStruct((n, t, B_WIDTH), BF16),
                   cache_shape, cache_shape],
        input_output_aliases=aliases,
        compiler_params=_cparams(("parallel", "parallel")),
    )(*args)


def _fox_prompt_kernel(tq, q_ref, k_ref, v_ref, o_ref, m_sc, l_sc, acc_sc):
    qi = pl.program_id(2)
    m_sc[...] = jnp.full(m_sc.shape, NEG_BIG, F32)
    l_sc[...] = jnp.zeros(l_sc.shape, F32)
    acc_sc[...] = jnp.zeros(acc_sc.shape, F32)
    n_tiles = tq // LANES
    heads = range(FOX_HEADS_PER_STEP)
    qw = 2 * HEAD_DIM

    def block(j, masked):
        off = pl.multiple_of(j * tq, tq)
        ss = [_dot_nt(q_ref[0, :, h * qw:(h + 1) * qw], k_ref[0, pl.ds(off, tq), h * qw:(h + 1) * qw])
              for h in heads]
        if masked:
            visible = _iota2((tq, tq), 1) <= _iota2((tq, tq), 0)
            ss = [jnp.where(visible, s, NEG_BIG) for s in ss]
        new = []
        for h, s in zip(heads, ss):
            tiles = [s[:, c * LANES:(c + 1) * LANES] for c in range(n_tiles)]
            mx = tiles[0]
            for tl in tiles[1:]:
                mx = jnp.maximum(mx, tl)
            m_prev = m_sc[h]
            m_new = jnp.maximum(m_prev, jnp.max(mx, axis=-1, keepdims=True))
            alpha = jnp.exp2(m_prev - m_new)
            ps = [jnp.exp2(tl - m_new) for tl in tiles]
            psum = ps[0]
            for pt in ps[1:]:
                psum = psum + pt
            p = jnp.concatenate([pt.astype(BF16) for pt in ps], axis=1)
            pv = _dot(p, v_ref[0, pl.ds(off, tq), h * HEAD_DIM:(h + 1) * HEAD_DIM])
            new.append((m_new, alpha * l_sc[h] + psum, alpha * acc_sc[h] + pv))
        for h, (m_new, l_new, acc_new) in zip(heads, new):
            m_sc[h] = m_new
            l_sc[h] = l_new
            acc_sc[h] = acc_new

    def body(j, carry):
        block(j, False)
        return carry

    lax.fori_loop(0, qi, body, 0)
    block(qi, True)
    for h in heads:
        o_ref[0, :, h * HEAD_DIM:(h + 1) * HEAD_DIM] = (
            acc_sc[h] / jnp.sum(l_sc[h], axis=-1, keepdims=True)).astype(BF16)


def _fox_prompt(qp, kp, vp, tq=512):
    n, t, _ = vp.shape
    tq = min(tq, t)
    hps = FOX_HEADS_PER_STEP
    return pl.pallas_call(
        functools.partial(_fox_prompt_kernel, tq),
        grid=(n, B_HEADS // hps, t // tq),
        in_specs=[pl.BlockSpec((1, tq, hps * 2 * HEAD_DIM), lambda i, h, qi: (i, qi, h)),
                  pl.BlockSpec((1, t, hps * 2 * HEAD_DIM), lambda i, h, qi: (i, 0, h)),
                  pl.BlockSpec((1, t, hps * HEAD_DIM), lambda i, h, qi: (i, 0, h))],
        out_specs=pl.BlockSpec((1, tq, hps * HEAD_DIM), lambda i, h, qi: (i, qi, h)),
        out_shape=jax.ShapeDtypeStruct((n, t, B_WIDTH), BF16),
        scratch_shapes=[pltpu.VMEM((hps, tq, LANES), F32),
                        pltpu.VMEM((hps, tq, LANES), F32),
                        pltpu.VMEM((hps, tq, HEAD_DIM), F32)],
        compiler_params=_cparams(("parallel", "parallel", "arbitrary")),
    )(qp, kp, vp)


def _fox_sample_kernel(t, p, q_ref, kn_ref, vn_ref, kp_ref, vp_ref, cq_ref, ckp_ref, ckn_ref, o_ref):
    causal = _iota2((t, t), 1) <= _iota2((t, t), 0)
    for h in range(B_HEADS):
        hs = slice(h * HEAD_DIM, (h + 1) * HEAD_DIM)
        kp = kp_ref[0, pl.ds(h, p, stride=B_HEADS), :].astype(BF16)
        vp = vp_ref[0, pl.ds(h, p, stride=B_HEADS), :].astype(BF16)
        qb = (q_ref[0, :, hs] * (HEAD_DIM ** -0.5)).astype(BF16)
        cq = cq_ref[0, h]
        sp = _dot_nt(qb, kp) + (cq - ckp_ref[0, h])
        sn = _dot_nt(qb, kn_ref[0, :, hs].astype(BF16)) + (cq - ckn_ref[0, h])
        sn = jnp.where(causal, sn, NEG_BIG)
        m = jnp.maximum(jnp.max(sp, axis=-1, keepdims=True), jnp.max(sn, axis=-1, keepdims=True))
        pp = jnp.exp(sp - m)
        pn = jnp.exp(sn - m)
        den = jnp.sum(pp, axis=-1, keepdims=True) + jnp.sum(pn, axis=-1, keepdims=True)
        o = _dot(pp.astype(BF16), vp) + _dot(pn.astype(BF16), vn_ref[0, :, hs].astype(BF16))
        o_ref[0, :, hs] = (o / den).astype(BF16)


def _fox_sample(hbig, k_past, v_past, layer, cq, ckp, ckn):
    n, t, _ = hbig.shape
    rows = k_past.shape[2]
    p = rows // B_HEADS
    col = lambda base: (lambda i: (i, 0, base // B_WIDTH))
    past = pl.BlockSpec((None, 1, rows, HEAD_DIM), lambda i: (layer, i, 0, 0))
    whole = lambda a: pl.BlockSpec((1,) + a.shape[1:], lambda i: (i, 0, 0, 0))
    return pl.pallas_call(
        functools.partial(_fox_sample_kernel, t, p),
        grid=(n,),
        in_specs=[pl.BlockSpec((1, t, B_WIDTH), col(COL_Q_B)),
                  pl.BlockSpec((1, t, B_WIDTH), col(COL_K_B)),
                  pl.BlockSpec((1, t, B_WIDTH), col(COL_V_B)),
                  past, past, whole(cq), whole(ckp), whole(ckn)],
        out_specs=pl.BlockSpec((1, t, B_WIDTH), lambda i: (i, 0, 0)),
        out_shape=jax.ShapeDtypeStruct((n, t, B_WIDTH), BF16),
        compiler_params=_cparams(("parallel",)),
    )(hbig, hbig, hbig, k_past, v_past, cq, ckp, ckn)


def _pad_cols(w, width):
    return jnp.pad(w, [(0, 0)] * (w.ndim - 1) + [(0, width - w.shape[-1])])


def _pad_heads_c(w):
    lead = w.shape[:-1]
    w = w.reshape(lead + (C_HEADS, C_DK))
    w = jnp.pad(w, [(0, 0)] * len(lead) + [(0, 0), (0, C_DK_PAD - C_DK)])
    return w.reshape(lead + (C_KEY_PAD,))


def _prep_w_in(w):
    o = 0
    parts = {}
    for name, size in (("qkv_a", A_CONV_DIM), ("z_a", A_VAL), ("a_a", A_HEADS), ("b_a", A_HEADS),
                       ("q_b", B_WIDTH), ("k_b", B_WIDTH), ("v_b", B_WIDTH), ("f_b", B_HEADS),
                       ("q_c", C_KEY), ("k_c", C_KEY), ("v_c", C_VAL), ("r_c", C_VAL), ("lr_c", C_RANK)):
        parts[name] = w[..., o:o + size].astype(BF16)
        o += size
    big = jnp.concatenate([parts["qkv_a"], parts["z_a"], parts["v_c"], parts["r_c"],
                           parts["q_b"], parts["k_b"], parts["v_b"],
                           _pad_heads_c(parts["q_c"]), _pad_heads_c(parts["k_c"])], axis=-1)
    gate = jnp.concatenate([_pad_cols(parts[nm], LANES) for nm in ("a_a", "b_a", "f_b", "lr_c")], axis=-1)
    return big, gate


def _pad_vec(v):
    return _pad_cols(v.reshape(1, -1).astype(F32), LANES)


def _mixer(xb, n, t, conv_buf8, s_a0, past_b, s_c0t, w, layer, depth, kv_caches):
    m = n * t
    hbig = _matmul(xb, w["w_big"], layer, 1024, 1024, F32).reshape(n, t, IN_BIG)
    hgate = _matmul(xb, w["w_gate4"], layer, 1024, IN_GATE, F32).reshape(n, t, IN_GATE)

    oa, s_a = _gdn(hbig, hgate, conv_buf8, w["conv_w"], w["a_log"], w["dt_bias"], w["a_norm_g"], s_a0)
    oc, s_ct = _gla(hbig, hgate, w["c_w2p"], w["c_b2p"], w["c_norm_g"], s_c0t)

    zero_c = jnp.zeros((n, 1, LANES), F32)
    if past_b is None:
        lf, c = _fgate(hgate, 2, w["f_bias"], zero_c, True, 256)
        qp, kp, vp, k_cache, v_cache = _fox_prep(hbig, c, layer, depth, kv_caches)
        kv_caches = (k_cache, v_cache)
        ob = _fox_prompt(qp, kp, vp)
        kb = vb = None
    else:
        k_past, v_past, lf_past = past_b
        p = lf_past.shape[1]
        lf_past = jnp.pad(lf_past.astype(F32), ((0, 0), (0, 0), (0, LANES - B_HEADS)))
        _, c_past = _fgate(lf_past, 0, w["f_bias"], zero_c, False, 256)
        lf, c = _fgate(hgate, 2, w["f_bias"], c_past[:, p - 1:p, :], True, t)
        cq = jnp.swapaxes(c[:, :, :B_HEADS], 1, 2)[..., None]
        ckn = jnp.swapaxes(c[:, :, :B_HEADS], 1, 2)[:, :, None, :]
        ckp = jnp.swapaxes(c_past[:, :, :B_HEADS], 1, 2)[:, :, None, :]
        ob = _fox_sample(hbig, k_past, v_past, layer, cq, ckp, ckn)

    y = _outproj(oa.reshape(m, A_VAL), ob.reshape(m, B_WIDTH), oc.reshape(m, C_VAL),
                 w["w_out"], layer, 1024, 1024)

    conv_new = hbig[:, t - (CONV_W - 1):, COL_QKV_A:COL_QKV_A + A_CONV_DIM]
    if past_b is not None:
        kb = hbig[:, :, COL_K_B:COL_K_B + B_WIDTH].reshape(n, t, B_HEADS, HEAD_DIM)
        vb = hbig[:, :, COL_V_B:COL_V_B + B_WIDTH].reshape(n, t, B_HEADS, HEAD_DIM)
    s_c = jnp.swapaxes(s_ct, 2, 3)[:, :, :C_DK, :]
    return y, (conv_new, s_a, kb, vb, lf[:, :, :B_HEADS], s_c), kv_caches


def _layer(x, xb, n, t, conv_buf8, s_a0, past_b, s_c0t, w, layer, depth, alpha, kv_caches=None):
    y, st, kv_caches = _mixer(xb, n, t, conv_buf8, s_a0, past_b, s_c0t, w, layer, depth, kv_caches)
    x, xb = _residual_ln(x, y, w["ln1_g"], w["ln1_b"], alpha)
    hmid = _swiglu(xb, w["w_gate"], w["w_up"], layer, 1024, 256)
    y2 = _matmul(hmid, w["w_down"], layer, 512, 512, F32)
    x, xb = _residual_ln(x, y2, w["ln2_g"], w["ln2_b"], alpha)
    return x, xb, st, kv_caches


def kernel(x_prompt, x_sample, state_a_conv, state_a_rec, cache_b_k, cache_b_v, cache_b_logf, state_c_rec, w_in, conv_w, a_log, dt_bias, a_norm_g, f_bias, c_w2, c_b2, c_norm_g, w_out, ln1_g, ln1_b, w_gate, w_up, w_down, ln2_g, ln2_b):
    depth = w_in.shape[0]
    alpha = (2 * depth) ** DEPTH_ALPHA_POW
    nb, tp, _ = x_prompt.shape
    ns, ts, _ = x_sample.shape

    hp = x_prompt.reshape(nb * tp, D_MODEL).astype(F32)
    hs = x_sample.reshape(ns * ts, D_MODEL).astype(F32)
    hpb = hp.astype(BF16)
    hsb = hs.astype(BF16)
    p_states, s_states = [], []
    w_big, w_gate4 = _prep_w_in(w_in)
    w_out_b, w_gate_b, w_up_b, w_down_b = (a.astype(BF16) for a in (w_out, w_gate, w_up, w_down))
    past_len = cache_b_k.shape[2]
    cache_k = cache_b_k.reshape(depth, ns, past_len * B_HEADS, HEAD_DIM).astype(F32)
    cache_v = cache_b_v.reshape(depth, ns, past_len * B_HEADS, HEAD_DIM).astype(F32)
    prompt_kv = None
    for l in range(depth):
        w = {
            "w_big": w_big, "w_gate4": w_gate4,
            "conv_w": conv_w[l].astype(F32),
            "a_log": _pad_vec(a_log[l]), "dt_bias": _pad_vec(dt_bias[l]),
            "a_norm_g": a_norm_g[l].reshape(1, HEAD_DIM).astype(F32),
            "f_bias": _pad_vec(f_bias[l]),
            "c_w2p": jnp.pad(_pad_heads_c(c_w2[l]), ((0, LANES - C_RANK), (0, 0))).astype(BF16),
            "c_b2p": _pad_heads_c(c_b2[l].reshape(1, C_KEY)).astype(F32),
            "c_norm_g": c_norm_g[l].reshape(1, C_DV).astype(F32),
            "w_out": w_out_b,
            "ln1_g": ln1_g[l], "ln1_b": ln1_b[l], "ln2_g": ln2_g[l], "ln2_b": ln2_b[l],
            "w_gate": w_gate_b, "w_up": w_up_b, "w_down": w_down_b,
        }
        hp, hpb, stp, prompt_kv = _layer(
            hp, hpb, nb, tp,
            jnp.zeros((nb, SUBLANES, A_CONV_DIM), F32),
            jnp.zeros((nb, A_HEADS, HEAD_DIM, HEAD_DIM), F32),
            None,
            jnp.zeros((nb, C_HEADS, C_DV, C_DK_PAD), F32), w, l, depth, alpha, prompt_kv)
        p_states.append(stp)
        buf8 = jnp.pad(state_a_conv[l].astype(F32), ((0, 0), (SUBLANES - (CONV_W - 1), 0), (0, 0)))
        s_c0t = jnp.pad(jnp.swapaxes(state_c_rec[l].astype(F32), 2, 3),
                        ((0, 0), (0, 0), (0, 0), (0, C_DK_PAD - C_DK)))
        hs, hsb, sts, _ = _layer(
            hs, hsb, ns, ts, buf8, state_a_rec[l].astype(F32),
            (cache_k, cache_v, cache_b_logf[l]), s_c0t, w, l, depth, alpha)
        s_states.append(sts)

    dp, ds = x_prompt.dtype, x_sample.dtype
    stack = lambda states, i, dt: jnp.stack([s[i] for s in states], axis=0).astype(dt)
    prompt_k, prompt_v = (a.reshape(depth, nb, tp, B_HEADS, HEAD_DIM).astype(dp) for a in prompt_kv)
    return ((hp.reshape(nb, tp, D_MODEL).astype(dp), hs.reshape(ns, ts, D_MODEL).astype(ds))
            + (stack(p_states, 0, dp), stack(p_states, 1, dp), prompt_k, prompt_v,
               stack(p_states, 4, dp), stack(p_states, 5, dp))
            + tuple(stack(s_states, i, ds) for i in range(6)))
```

```python
import functools

import jax
import jax.numpy as jnp
from jax import lax
from jax.experimental import pallas as pl
from jax.experimental.pallas import tpu as pltpu

F32 = jnp.float32
BF16 = jnp.bfloat16

D_MODEL = 4096
CHUNK = 64
HEAD_DIM = 128
A_HEADS = 12
A_KEY = A_HEADS * HEAD_DIM
A_VAL = A_HEADS * HEAD_DIM
A_CONV_DIM = 2 * A_KEY + A_VAL
CONV_W = 4
B_HEADS = 8
B_WIDTH = B_HEADS * HEAD_DIM
C_HEADS = 4
C_DV = 384
C_DK = 192
C_DK_PAD = 256
C_KEY = C_HEADS * C_DK
C_KEY_PAD = C_HEADS * C_DK_PAD
C_VAL = C_HEADS * C_DV
C_RANK = 16
C_TAU = 16.0
D_FF = 11008
DEPTH_ALPHA_POW = 0.25
LN_EPS = 1e-5
RMS_EPS = 1e-6

LANES = 128
SUBLANES = 8
VMEM_LIMIT = 56 * 1024 * 1024

COL_QKV_A = 0
COL_Z_A = 4608
COL_V_C = 6144
COL_R_C = 7680
COL_Q_B = 9216
COL_K_B = 10240
COL_V_B = 11264
COL_Q_C = 12288
COL_K_C = 13312
IN_BIG = 14336
GATE_BLOCKS = 4
IN_GATE = GATE_BLOCKS * LANES

NEG_BIG = -1e30
LOG2E = 1.4426950408889634
FOX_HEADS_PER_STEP = 4
GDN_SEQS_PER_STEP = 1
GDN_HEADS_PER_GROUP = 3


def _cparams(sem):
    return pltpu.CompilerParams(dimension_semantics=sem, vmem_limit_bytes=VMEM_LIMIT)


def _sigmoid(x):
    return 1.0 / (1.0 + jnp.exp(-x))


def _silu(x):
    return x * _sigmoid(x)


def _softplus(x):
    return jnp.maximum(x, 0.0) + jnp.log(1.0 + jnp.exp(-jnp.abs(x)))


def _log_sigmoid(x):
    return -_softplus(-x)


def _split3(x):
    x1 = x.astype(BF16)
    r1 = x - x1.astype(F32)
    x2 = r1.astype(BF16)
    x3 = (r1 - x2.astype(F32)).astype(BF16)
    return x1, x2, x3


def _dot(a, b):
    return jnp.dot(a, b, preferred_element_type=F32)


def _dot_nt(a, b):
    return lax.dot_general(a, b, (((1,), (1,)), ((), ())), preferred_element_type=F32)


def _dot_tn(a, b):
    return lax.dot_general(a, b, (((0,), (0,)), ((), ())), preferred_element_type=F32)


def _dot_exact_lhs(a_bf16, x):
    x1, x2, x3 = _split3(x)
    return _dot(a_bf16, x1) + _dot(a_bf16, x2) + _dot(a_bf16, x3)


def _iota2(shape, dim):
    return lax.broadcasted_iota(jnp.int32, shape, dim)


def _tri_incl(n):
    return (_iota2((n, n), 1) <= _iota2((n, n), 0)).astype(BF16)


def _mm_kernel(x_ref, w_ref, o_ref):
    o_ref[...] = _dot(x_ref[...], w_ref[...]).astype(o_ref.dtype)


def _matmul(x, w, layer, tm, tn, out_dtype):
    m, k = x.shape
    n = w.shape[2]
    tm = min(tm, m)
    return pl.pallas_call(
        _mm_kernel,
        grid=(m // tm, n // tn),
        in_specs=[pl.BlockSpec((tm, k), lambda i, j: (i, 0)),
                  pl.BlockSpec((None, k, tn), lambda i, j: (layer, 0, j))],
        out_specs=pl.BlockSpec((tm, tn), lambda i, j: (i, j)),
        out_shape=jax.ShapeDtypeStruct((m, n), out_dtype),
        compiler_params=_cparams(("parallel", "parallel")),
    )(x, w)


def _lane_partial(v):
    part = v[:, 0:LANES]
    for c in range(1, v.shape[1] // LANES):
        part = part + v[:, c * LANES:(c + 1) * LANES]
    return part


def _residual_epilogue(alpha, r, acc, v_ref, s1_ref, s2_ref):
    v = alpha * r + acc
    v_ref[...] = v
    p1 = _lane_partial(v)
    p2 = _lane_partial(v * v)
    first = pl.program_id(1) == 0
    s1_ref[...] = jnp.where(first, p1, s1_ref[...] + p1)
    s2_ref[...] = jnp.where(first, p2, s2_ref[...] + p2)


def _row_stats(s1, s2, width):
    mu = jnp.sum(s1, axis=-1, keepdims=True) * (1.0 / width)
    var = jnp.sum(s2, axis=-1, keepdims=True) * (1.0 / width) - mu * mu
    return mu, lax.rsqrt(var + LN_EPS)


def _outproj_kernel(alpha, a_ref, b_ref, c_ref, w_ref, r_ref, v_ref, s1_ref, s2_ref):
    acc = (_dot(a_ref[...], w_ref[0:A_VAL, :])
           + _dot(b_ref[...], w_ref[A_VAL:A_VAL + B_WIDTH, :])
           + _dot(c_ref[...], w_ref[A_VAL + B_WIDTH:D_MODEL, :]))
    _residual_epilogue(alpha, r_ref[...], acc, v_ref, s1_ref, s2_ref)


def _resid_out(m, tm, tn):
    tile = pl.BlockSpec((tm, tn), lambda i, j: (i, j))
    stat = pl.BlockSpec((tm, LANES), lambda i, j: (i, 0))
    return ([tile, stat, stat],
            [jax.ShapeDtypeStruct((m, D_MODEL), F32)] + [jax.ShapeDtypeStruct((m, LANES), F32)] * 2)


def _outproj(oa, ob, oc, w, layer, resid, alpha, tm, tn):
    m = oa.shape[0]
    tm = min(tm, m)
    rows = lambda width: pl.BlockSpec((tm, width), lambda i, j: (i, 0))
    out_specs, out_shape = _resid_out(m, tm, tn)
    return pl.pallas_call(
        functools.partial(_outproj_kernel, alpha),
        grid=(m // tm, D_MODEL // tn),
        in_specs=[rows(A_VAL), rows(B_WIDTH), rows(C_VAL),
                  pl.BlockSpec((None, D_MODEL, tn), lambda i, j: (layer, 0, j)),
                  pl.BlockSpec((tm, tn), lambda i, j: (i, j))],
        out_specs=out_specs, out_shape=out_shape,
        compiler_params=_cparams(("parallel", "arbitrary")),
    )(oa, ob, oc, w, resid)


def _down_kernel(alpha, h_ref, w_ref, v1_ref, s1_ref, s2_ref, g_ref, b_ref, v_ref, t1_ref, t2_ref):
    acc = _dot(h_ref[...], w_ref[...])
    mu, rstd = _row_stats(s1_ref[...], s2_ref[...], D_MODEL)
    x1 = (v1_ref[...] - mu) * rstd * g_ref[...] + b_ref[...]
    _residual_epilogue(alpha, x1, acc, v_ref, t1_ref, t2_ref)


def _down_proj(hmid, w, layer, v1, s1, s2, g, b, alpha, tm, tn):
    m, k = hmid.shape
    tm = min(tm, m)
    tile = pl.BlockSpec((tm, tn), lambda i, j: (i, j))
    stat = pl.BlockSpec((tm, LANES), lambda i, j: (i, 0))
    vec = pl.BlockSpec((1, tn), lambda i, j: (0, j))
    out_specs, out_shape = _resid_out(m, tm, tn)
    return pl.pallas_call(
        functools.partial(_down_kernel, alpha),
        grid=(m // tm, D_MODEL // tn),
        in_specs=[pl.BlockSpec((tm, k), lambda i, j: (i, 0)),
                  pl.BlockSpec((None, k, tn), lambda i, j: (layer, 0, j)),
                  tile, stat, stat, vec, vec],
        out_specs=out_specs, out_shape=out_shape,
        compiler_params=_cparams(("parallel", "arbitrary")),
    )(hmid, w, v1, s1, s2, g.reshape(1, D_MODEL), b.reshape(1, D_MODEL))


def _swiglu_kernel(x_ref, wg_ref, wu_ref, o_ref):
    x = x_ref[...]
    a = _dot(x, wg_ref[...])
    b = _dot(x, wu_ref[...])
    o_ref[...] = (_silu(a) * b).astype(o_ref.dtype)


def _swiglu(x, wg, wu, layer, tm, tn):
    m, k = x.shape
    n = wg.shape[2]
    tm = min(tm, m)
    wspec = pl.BlockSpec((None, k, tn), lambda i, j: (layer, 0, j))
    return pl.pallas_call(
        _swiglu_kernel,
        grid=(m // tm, n // tn),
        in_specs=[pl.BlockSpec((tm, k), lambda i, j: (i, 0)), wspec, wspec],
        out_specs=pl.BlockSpec((tm, tn), lambda i, j: (i, j)),
        out_shape=jax.ShapeDtypeStruct((m, n), BF16),
        compiler_params=_cparams(("parallel", "parallel")),
    )(x, wg, wu)


def _ln_apply_kernel(with_f32, v_ref, s1_ref, s2_ref, g_ref, b_ref, *out_refs):
    mu, rstd = _row_stats(s1_ref[...], s2_ref[...], D_MODEL)
    out = (v_ref[...] - mu) * rstd * g_ref[...] + b_ref[...]
    out_refs[0][...] = out.astype(BF16)
    if with_f32:
        out_refs[1][...] = out


def _ln_apply(v, s1, s2, g, b, with_f32, tm=256):
    m, d = v.shape
    tm = min(tm, m)
    row = pl.BlockSpec((tm, d), lambda i: (i, 0))
    stat = pl.BlockSpec((tm, LANES), lambda i: (i, 0))
    vec = pl.BlockSpec((1, d), lambda i: (0, 0))
    n_out = 2 if with_f32 else 1
    return pl.pallas_call(
        functools.partial(_ln_apply_kernel, with_f32),
        grid=(m // tm,),
        in_specs=[row, stat, stat, vec, vec],
        out_specs=[row] * n_out,
        out_shape=[jax.ShapeDtypeStruct((m, d), BF16), jax.ShapeDtypeStruct((m, d), F32)][:n_out],
        compiler_params=_cparams(("parallel",)),
    )(v, s1, s2, g.reshape(1, d), b.reshape(1, d))


def _fgate_kernel(apply_gate, tb, x_ref, bias_ref, c0_ref, lf_ref, c_ref, carry):
    @pl.when(pl.program_id(1) == 0)
    def _():
        carry[...] = c0_ref[0]

    x = x_ref[0]
    lf = _log_sigmoid(x + bias_ref[...]) if apply_gate else x
    c = _dot_exact_lhs(_tri_incl(tb), lf) + carry[...]
    lf_ref[0] = lf
    c_ref[0] = c
    carry[...] = c[tb - 1:tb, :]


def _fgate(x, col_block, bias, c0, apply_gate, tb):
    n, t, _ = x.shape
    tb = min(tb, t)
    blk = pl.BlockSpec((1, tb, LANES), lambda i, j: (i, j, 0))
    return pl.pallas_call(
        functools.partial(_fgate_kernel, apply_gate, tb),
        grid=(n, t // tb),
        in_specs=[pl.BlockSpec((1, tb, LANES), lambda i, j: (i, j, col_block)),
                  pl.BlockSpec((1, LANES), lambda i, j: (0, 0)),
                  pl.BlockSpec((1, 1, LANES), lambda i, j: (i, 0, 0))],
        out_specs=[blk, blk],
        out_shape=[jax.ShapeDtypeStruct((n, t, LANES), F32)] * 2,
        scratch_shapes=[pltpu.VMEM((1, LANES), F32)],
        compiler_params=_cparams(("parallel", "arbitrary")),
    )(x, bias, c0)


def _gdn_kernel(L, NB, qkv_ref, z_ref, gate_ref, buf_ref, convw_ref, alog_ref, dtb_ref, ng_ref, s0_ref,
                o_ref, s_ref, xs):
    @pl.when(pl.program_id(1) == 0)
    def _():
        xs[:, 0:SUBLANES, :] = buf_ref[...]
        s_ref[...] = s0_ref[...]

    xs[:, SUBLANES:SUBLANES + L, :] = qkv_ref[...]

    def conv_cols(b, c0):
        acc = xs[b, 5:5 + L, c0:c0 + HEAD_DIM] * convw_ref[0:1, c0:c0 + HEAD_DIM]
        for i in range(1, CONV_W):
            acc = acc + xs[b, 5 + i:5 + i + L, c0:c0 + HEAD_DIM] * convw_ref[i:i + 1, c0:c0 + HEAD_DIM]
        return _silu(acc)

    def l2n(x):
        return x * lax.rsqrt(jnp.sum(x * x, axis=-1, keepdims=True) + RMS_EPS)

    tri = _tri_incl(L)
    eye_l = (_iota2((LANES, LANES), 0) == _iota2((LANES, LANES), 1)).astype(BF16)
    seqs = []
    for b in range(NB):
        a_in = gate_ref[b, :, 0:LANES]
        b_in = gate_ref[b, :, LANES:2 * LANES]
        g = -jnp.exp(alog_ref[...]) * _softplus(a_in + dtb_ref[...])
        gc = _dot_exact_lhs(tri, g)
        g1, g2, g3 = _split3(gc)
        gct = _dot_nt(eye_l, g1) + _dot_nt(eye_l, g2) + _dot_nt(eye_l, g3)
        seqs.append(dict(gc=gc, gct=gct, beta=_sigmoid(b_in)))

    row = _iota2((L, L), 0)
    col = _iota2((L, L), 1)
    eye = (row == col).astype(F32)
    n_lvl = L.bit_length() - 2

    def mm(a, b):
        return _dot(a.astype(BF16), b.astype(BF16))

    def st_prep(hd):
        b, h = hd["b"], hd["h"]
        sq = seqs[b]
        q = l2n(conv_cols(b, h * HEAD_DIM)) * (HEAD_DIM ** -0.5)
        k = l2n(conv_cols(b, A_KEY + h * HEAD_DIM))
        v = conv_cols(b, 2 * A_KEY + h * HEAD_DIM)
        gcol = sq["gc"][:, h:h + 1]
        bcol = sq["beta"][:, h:h + 1]
        glast = sq["gc"][L - 1:L, h:h + 1]
        dec = jnp.exp(jnp.where(row >= col, gcol - sq["gct"][h:h + 1, :], NEG_BIG))
        eg = jnp.exp(gcol)
        qb = q.astype(BF16)
        kb = k.astype(BF16)
        p = -(bcol * _dot_nt(kb, kb) * jnp.where(row > col, dec, 0.0))
        hd.update(qb=qb, p=p, t=eye + p, eg=eg, e_last=jnp.exp(glast),
                  qkd=(_dot_nt(qb, kb) * dec).astype(BF16),
                  rhs=jnp.concatenate([(bcol * eg) * k, bcol * v], axis=1),
                  k_end=(k * jnp.exp(glast - gcol)).astype(BF16),
                  s=s_ref[b, h])

    def st_square(hd):
        hd["p"] = mm(hd["p"], hd["p"])

    def st_level(hd):
        pt = mm(hd["p"], jnp.concatenate([hd["p"], hd["t"]], axis=1))
        hd["t"] = hd["t"] + pt[:, L:2 * L]
        hd["p"] = pt[:, 0:L]

    def st_last_level(hd):
        hd["t"] = hd["t"] + mm(hd["p"], hd["t"])

    def st_solve(hd):
        hd["tr"] = mm(hd["t"], hd["rhs"])

    def st_state(hd):
        tr = hd["tr"]
        sb = hd["s"].astype(BF16)
        hd["ub"] = (tr[:, HEAD_DIM:2 * HEAD_DIM] - _dot(tr[:, 0:HEAD_DIM].astype(BF16), sb)).astype(BF16)
        hd["qs"] = _dot(hd["qb"], sb)

    def st_out(hd):
        o = hd["eg"] * hd["qs"] + _dot(hd["qkd"], hd["ub"])
        hd["s_new"] = hd["e_last"] * hd["s"] + _dot_tn(hd["k_end"], hd["ub"])
        hd["o"] = o * lax.rsqrt(jnp.mean(o * o, axis=-1, keepdims=True) + RMS_EPS) * ng_ref[...]

    stages = [st_prep, st_square] + [st_level] * (n_lvl - 1) + [st_last_level, st_solve, st_state, st_out]
    heads = [dict(b=b, h=h) for b in range(NB) for h in range(A_HEADS)]
    groups = [heads[i:i + GDN_HEADS_PER_GROUP] for i in range(0, len(heads), GDN_HEADS_PER_GROUP)]
    for wave in range(len(groups) + len(stages) - 1):
        for g, group in enumerate(groups):
            if 0 <= wave - g < len(stages):
                for hd in group:
                    stages[wave - g](hd)

    for hd in heads:
        b, h = hd["b"], hd["h"]
        s_ref[b, h] = hd["s_new"]
        zh = z_ref[b, :, h * HEAD_DIM:(h + 1) * HEAD_DIM]
        o_ref[b, :, h * HEAD_DIM:(h + 1) * HEAD_DIM] = (hd["o"] * _silu(zh)).astype(BF16)

    tail = xs[:, L:L + SUBLANES, :]
    xs[:, 0:SUBLANES, :] = tail


def _gdn(hbig, hgate, conv_buf8, conv_w, a_log, dt_bias, norm_g, s0):
    n, t, _ = hbig.shape
    L = min(t, CHUNK)
    nb = GDN_SEQS_PER_STEP
    const2 = lambda i, j: (0, 0)
    state = pl.BlockSpec((nb, A_HEADS, HEAD_DIM, HEAD_DIM), lambda i, j: (i, 0, 0, 0))
    return pl.pallas_call(
        functools.partial(_gdn_kernel, L, nb),
        grid=(n // nb, t // L),
        in_specs=[pl.BlockSpec((nb, L, A_CONV_DIM), lambda i, j: (i, j, COL_QKV_A // A_CONV_DIM)),
                  pl.BlockSpec((nb, L, A_VAL), lambda i, j: (i, j, COL_Z_A // A_VAL)),
                  pl.BlockSpec((nb, L, 2 * LANES), lambda i, j: (i, j, 0)),
                  pl.BlockSpec((nb, SUBLANES, A_CONV_DIM), lambda i, j: (i, 0, 0)),
                  pl.BlockSpec((CONV_W, A_CONV_DIM), const2),
                  pl.BlockSpec((1, LANES), const2),
                  pl.BlockSpec((1, LANES), const2),
                  pl.BlockSpec((1, HEAD_DIM), const2),
                  state],
        out_specs=[pl.BlockSpec((nb, L, A_VAL), lambda i, j: (i, j, 0)), state],
        out_shape=[jax.ShapeDtypeStruct((n, t, A_VAL), BF16),
                   jax.ShapeDtypeStruct((n, A_HEADS, HEAD_DIM, HEAD_DIM), F32)],
        scratch_shapes=[pltpu.VMEM((nb, L + SUBLANES, A_CONV_DIM), F32)],
        compiler_params=_cparams(("parallel", "arbitrary")),
    )(hbig, hbig, hgate, conv_buf8, conv_w, a_log, dt_bias, norm_g, s0)


def _gla_kernel(L, q_ref, k_ref, v_ref, r_ref, gate_ref, w2_ref, b2_ref, ng_ref, s0_ref,
                o_ref, s_ref):
    @pl.when(pl.program_id(1) == 0)
    def _():
        s_ref[0] = s0_ref[0]

    z = _dot(gate_ref[0].astype(BF16), w2_ref[...]) + b2_ref[...]
    lg = _log_sigmoid(z) * (1.0 / C_TAU)
    b = _dot_exact_lhs(_tri_incl(L), lg)
    causal = _iota2((L, L), 0) >= _iota2((L, L), 1)

    heads = []
    for h in range(C_HEADS):
        ks = slice(h * C_DK_PAD, (h + 1) * C_DK_PAD)
        vs = slice(h * C_DV, (h + 1) * C_DV)
        bh = b[:, ks]
        blast = bh[L - 1:L, :]
        bref = bh[L // 2:L // 2 + 1, :]
        q = q_ref[0, :, ks] * (C_DK ** -0.5)
        k = k_ref[0, :, ks]
        heads.append(dict(
            vs=vs, vb=v_ref[0, :, vs].astype(BF16), st=s_ref[0, h],
            q_state=(q * jnp.exp(bh)).astype(BF16),
            q_in=(q * jnp.exp(bh - bref)).astype(BF16),
            k_in=(k * jnp.exp(bref - bh)).astype(BF16),
            k_end=(k * jnp.exp(blast - bh)).astype(BF16),
            e_last=jnp.exp(blast)))
    for hd in heads:
        hd["att"] = jnp.where(causal, _dot_nt(hd["q_in"], hd["k_in"]), 0.0).astype(BF16)
        hd["o"] = _dot_nt(hd["q_state"], hd["st"].astype(BF16))
        hd["st_new"] = hd["st"] * hd["e_last"] + _dot_tn(hd["vb"], hd["k_end"])
    for hd in heads:
        o = hd["o"] + _dot(hd["att"], hd["vb"])
        hd["o"] = o * lax.rsqrt(jnp.mean(o * o, axis=-1, keepdims=True) + RMS_EPS) * ng_ref[...]
    for h, hd in enumerate(heads):
        s_ref[0, h] = hd["st_new"]
        o_ref[0, :, hd["vs"]] = (hd["o"] * _silu(r_ref[0, :, hd["vs"]])).astype(BF16)


def _gla(hbig, hgate, w2p, b2p, norm_g, s0t):
    n, t, _ = hbig.shape
    L = min(t, CHUNK)
    const2 = lambda i, j: (0, 0)
    state = pl.BlockSpec((1, C_HEADS, C_DV, C_DK_PAD), lambda i, j: (i, 0, 0, 0))
    return pl.pallas_call(
        functools.partial(_gla_kernel, L),
        grid=(n, t // L),
        in_specs=[pl.BlockSpec((1, L, C_KEY_PAD), lambda i, j: (i, j, COL_Q_C // C_KEY_PAD)),
                  pl.BlockSpec((1, L, C_KEY_PAD), lambda i, j: (i, j, COL_K_C // C_KEY_PAD)),
                  pl.BlockSpec((1, L, C_VAL), lambda i, j: (i, j, COL_V_C // C_VAL)),
                  pl.BlockSpec((1, L, C_VAL), lambda i, j: (i, j, COL_R_C // C_VAL)),
                  pl.BlockSpec((1, L, LANES), lambda i, j: (i, j, 3)),
                  pl.BlockSpec((LANES, C_KEY_PAD), const2),
                  pl.BlockSpec((1, C_KEY_PAD), const2),
                  pl.BlockSpec((1, C_DV), const2),
                  state],
        out_specs=[pl.BlockSpec((1, L, C_VAL), lambda i, j: (i, j, 0)), state],
        out_shape=[jax.ShapeDtypeStruct((n, t, C_VAL), BF16),
                   jax.ShapeDtypeStruct((n, C_HEADS, C_DV, C_DK_PAD), F32)],
        compiler_params=_cparams(("parallel", "arbitrary")),
    )(hbig, hbig, hbig, hbig, hgate, w2p, b2p, norm_g, s0t)


def _fox_prep_kernel(q_ref, k_ref, v_ref, c_ref, qp_ref, kp_ref, vp_ref):
    tb = q_ref.shape[1]
    lane = _iota2((tb, LANES), 1)
    ones_q = jnp.where((lane >= 3) & (lane < 6), 1.0, 0.0)
    ones_k = jnp.where(lane < 3, 1.0, 0.0)
    c2 = c_ref[0] * LOG2E
    for h in range(B_HEADS):
        hs = slice(h * HEAD_DIM, (h + 1) * HEAD_DIM)
        c1, cm, cl = (t.astype(F32) for t in _split3(c2[:, h:h + 1]))
        ext_q = jnp.where(lane == 0, c1, jnp.where(lane == 1, cm, jnp.where(lane == 2, cl, ones_q)))
        ext_k = jnp.where(lane == 3, -c1, jnp.where(lane == 4, -cm, jnp.where(lane == 5, -cl, ones_k)))
        base = 2 * h * HEAD_DIM
        qp_ref[0, :, base:base + HEAD_DIM] = (q_ref[0, :, hs] * (HEAD_DIM ** -0.5 * LOG2E)).astype(BF16)
        qp_ref[0, :, base + HEAD_DIM:base + 2 * HEAD_DIM] = ext_q.astype(BF16)
        kp_ref[0, :, base:base + HEAD_DIM] = k_ref[0, :, hs].astype(BF16)
        kp_ref[0, :, base + HEAD_DIM:base + 2 * HEAD_DIM] = ext_k.astype(BF16)
    vp_ref[0] = v_ref[0].astype(BF16)


def _fox_prep_cache_kernel(q_ref, k_ref, v_ref, c_ref, *rest):
    qp_ref, kp_ref, vp_ref, kc_ref, vc_ref = rest[-5:]
    _fox_prep_kernel(q_ref, k_ref, v_ref, c_ref, qp_ref, kp_ref, vp_ref)
    kc_ref[0] = k_ref[0]
    vc_ref[0] = v_ref[0]


def _fox_prep(hbig, c, layer, depth, caches, tb=256):
    n, t, _ = hbig.shape
    tb = min(tb, t)
    col = lambda blk: (lambda i, j: (i, j, blk))
    wide = pl.BlockSpec((1, tb, 2 * B_WIDTH), col(0))
    cache_spec = pl.BlockSpec((None, 1, tb, B_WIDTH), lambda i, j: (layer, i, j, 0))
    cache_shape = jax.ShapeDtypeStruct((depth, n, t, B_WIDTH), F32)
    in_specs = [pl.BlockSpec((1, tb, B_WIDTH), col(COL_Q_B // B_WIDTH)),
                pl.BlockSpec((1, tb, B_WIDTH), col(COL_K_B // B_WIDTH)),
                pl.BlockSpec((1, tb, B_WIDTH), col(COL_V_B // B_WIDTH)),
                pl.BlockSpec((1, tb, LANES), col(0))]
    args = [hbig, hbig, hbig, c]
    aliases = {}
    if caches is not None:
        in_specs += [pl.BlockSpec(memory_space=pl.ANY)] * 2
        args += list(caches)
        aliases = {4: 3, 5: 4}
    return pl.pallas_call(
        _fox_prep_cache_kernel,
        grid=(n, t // tb),
        in_specs=in_specs,
        out_specs=[wide, wide, pl.BlockSpec((1, tb, B_WIDTH), col(0)), cache_spec, cache_spec],
        out_shape=[jax.ShapeDtypeStruct((n, t, 2 * B_WIDTH), BF16),
                   jax.ShapeDtypeStruct((n, t, 2 * B_WIDTH), BF16),
                   jax.ShapeDtypeStruct((n, t, B_WIDTH), BF16),
                   cache_shape, cache_shape],
        input_output_aliases=aliases,
        compiler_params=_cparams(("parallel", "parallel")),
    )(*args)


def _fox_prompt_kernel(tq, q_ref, k_ref, v_ref, o_ref, m_sc, l_sc, acc_sc):
    qi = pl.program_id(2)
    m_sc[...] = jnp.full(m_sc.shape, NEG_BIG, F32)
    l_sc[...] = jnp.zeros(l_sc.shape, F32)
    acc_sc[...] = jnp.zeros(acc_sc.shape, F32)
    n_tiles = tq // LANES
    heads = range(FOX_HEADS_PER_STEP)
    qw = 2 * HEAD_DIM

    def block(j, masked):
        off = pl.multiple_of(j * tq, tq)
        ss = [_dot_nt(q_ref[0, :, h * qw:(h + 1) * qw], k_ref[0, pl.ds(off, tq), h * qw:(h + 1) * qw])
              for h in heads]
        if masked:
            visible = _iota2((tq, tq), 1) <= _iota2((tq, tq), 0)
            ss = [jnp.where(visible, s, NEG_BIG) for s in ss]
        new = []
        for h, s in zip(heads, ss):
            tiles = [s[:, c * LANES:(c + 1) * LANES] for c in range(n_tiles)]
            mx = tiles[0]
            for tl in tiles[1:]:
                mx = jnp.maximum(mx, tl)
            m_prev = m_sc[h]
            m_new = jnp.maximum(m_prev, jnp.max(mx, axis=-1, keepdims=True))
            alpha = jnp.exp2(m_prev - m_new)
            ps = [jnp.exp2(tl - m_new) for tl in tiles]
            psum = ps[0]
            for pt in ps[1:]:
                psum = psum + pt
            p = jnp.concatenate([pt.astype(BF16) for pt in ps], axis=1)
            pv = _dot(p, v_ref[0, pl.ds(off, tq), h * HEAD_DIM:(h + 1) * HEAD_DIM])
            new.append((m_new, alpha * l_sc[h] + psum, alpha * acc_sc[h] + pv))
        for h, (m_new, l_new, acc_new) in zip(heads, new):
            m_sc[h] = m_new
            l_sc[h] = l_new
            acc_sc[h] = acc_new

    def body(j, carry):
        block(j, False)
        return carry

    lax.fori_loop(0, qi, body, 0)
    block(qi, True)
    for h in heads:
        o_ref[0, :, h * HEAD_DIM:(h + 1) * HEAD_DIM] = (
            acc_sc[h] / jnp.sum(l_sc[h], axis=-1, keepdims=True)).astype(BF16)


def _fox_prompt(qp, kp, vp, tq=512):
    n, t, _ = vp.shape
    tq = min(tq, t)
    hps = FOX_HEADS_PER_STEP
    return pl.pallas_call(
        functools.partial(_fox_prompt_kernel, tq),
        grid=(n, B_HEADS // hps, t // tq),
        in_specs=[pl.BlockSpec((1, tq, hps * 2 * HEAD_DIM), lambda i, h, qi: (i, qi, h)),
                  pl.BlockSpec((1, t, hps * 2 * HEAD_DIM), lambda i, h, qi: (i, 0, h),
                               pipeline_mode=pl.Buffered(1)),
                  pl.BlockSpec((1, t, hps * HEAD_DIM), lambda i, h, qi: (i, 0, h),
                               pipeline_mode=pl.Buffered(1))],
        out_specs=pl.BlockSpec((1, tq, hps * HEAD_DIM), lambda i, h, qi: (i, qi, h)),
        out_shape=jax.ShapeDtypeStruct((n, t, B_WIDTH), BF16),
        scratch_shapes=[pltpu.VMEM((hps, tq, LANES), F32),
                        pltpu.VMEM((hps, tq, LANES), F32),
                        pltpu.VMEM((hps, tq, HEAD_DIM), F32)],
        compiler_params=_cparams(("parallel", "parallel", "arbitrary")),
    )(qp, kp, vp)


def _fox_sample_kernel(t, p, q_ref, kn_ref, vn_ref, kp_ref, vp_ref, cq_ref, ckp_ref, ckn_ref, o_ref):
    causal = _iota2((t, t), 1) <= _iota2((t, t), 0)
    for h in range(B_HEADS):
        hs = slice(h * HEAD_DIM, (h + 1) * HEAD_DIM)
        kp = kp_ref[0, pl.ds(h, p, stride=B_HEADS), :].astype(BF16)
        vp = vp_ref[0, pl.ds(h, p, stride=B_HEADS), :].astype(BF16)
        qb = (q_ref[0, :, hs] * (HEAD_DIM ** -0.5)).astype(BF16)
        cq = cq_ref[0, h]
        sp = _dot_nt(qb, kp) + (cq - ckp_ref[0, h])
        sn = _dot_nt(qb, kn_ref[0, :, hs].astype(BF16)) + (cq - ckn_ref[0, h])
        sn = jnp.where(causal, sn, NEG_BIG)
        m = jnp.maximum(jnp.max(sp, axis=-1, keepdims=True), jnp.max(sn, axis=-1, keepdims=True))
        pp = jnp.exp(sp - m)
        pn = jnp.exp(sn - m)
        den = jnp.sum(pp, axis=-1, keepdims=True) + jnp.sum(pn, axis=-1, keepdims=True)
        o = _dot(pp.astype(BF16), vp) + _dot(pn.astype(BF16), vn_ref[0, :, hs].astype(BF16))
        o_ref[0, :, hs] = (o / den).astype(BF16)


def _fox_sample(hbig, k_past, v_past, layer, cq, ckp, ckn):
    n, t, _ = hbig.shape
    rows = k_past.shape[2]
    p = rows // B_HEADS
    col = lambda base: (lambda i: (i, 0, base // B_WIDTH))
    past = pl.BlockSpec((None, 1, rows, HEAD_DIM), lambda i: (layer, i, 0, 0))
    whole = lambda a: pl.BlockSpec((1,) + a.shape[1:], lambda i: (i, 0, 0, 0))
    return pl.pallas_call(
        functools.partial(_fox_sample_kernel, t, p),
        grid=(n,),
        in_specs=[pl.BlockSpec((1, t, B_WIDTH), col(COL_Q_B)),
                  pl.BlockSpec((1, t, B_WIDTH), col(COL_K_B)),
                  pl.BlockSpec((1, t, B_WIDTH), col(COL_V_B)),
                  past, past, whole(cq), whole(ckp), whole(ckn)],
        out_specs=pl.BlockSpec((1, t, B_WIDTH), lambda i: (i, 0, 0)),
        out_shape=jax.ShapeDtypeStruct((n, t, B_WIDTH), BF16),
        compiler_params=_cparams(("parallel",)),
    )(hbig, hbig, hbig, k_past, v_past, cq, ckp, ckn)


def _pad_cols(w, width):
    return jnp.pad(w, [(0, 0)] * (w.ndim - 1) + [(0, width - w.shape[-1])])


def _pad_heads_c(w):
    lead = w.shape[:-1]
    w = w.reshape(lead + (C_HEADS, C_DK))
    w = jnp.pad(w, [(0, 0)] * len(lead) + [(0, 0), (0, C_DK_PAD - C_DK)])
    return w.reshape(lead + (C_KEY_PAD,))


def _prep_w_in(w):
    o = 0
    parts = {}
    for name, size in (("qkv_a", A_CONV_DIM), ("z_a", A_VAL), ("a_a", A_HEADS), ("b_a", A_HEADS),
                       ("q_b", B_WIDTH), ("k_b", B_WIDTH), ("v_b", B_WIDTH), ("f_b", B_HEADS),
                       ("q_c", C_KEY), ("k_c", C_KEY), ("v_c", C_VAL), ("r_c", C_VAL), ("lr_c", C_RANK)):
        parts[name] = w[..., o:o + size].astype(BF16)
        o += size
    big = jnp.concatenate([parts["qkv_a"], parts["z_a"], parts["v_c"], parts["r_c"],
                           parts["q_b"], parts["k_b"], parts["v_b"],
                           _pad_heads_c(parts["q_c"]), _pad_heads_c(parts["k_c"])], axis=-1)
    gate = jnp.concatenate([_pad_cols(parts[nm], LANES) for nm in ("a_a", "b_a", "f_b", "lr_c")], axis=-1)
    return big, gate


def _pad_vec(v):
    return _pad_cols(v.reshape(1, -1).astype(F32), LANES)


def _mixer(x, xb, n, t, conv_buf8, s_a0, past_b, s_c0t, w, layer, depth, alpha, kv_caches):
    m = n * t
    hbig = _matmul(xb, w["w_big"], layer, 1024, 1024, F32).reshape(n, t, IN_BIG)
    hgate = _matmul(xb, w["w_gate4"], layer, 1024, IN_GATE, F32).reshape(n, t, IN_GATE)

    oa, s_a = _gdn(hbig, hgate, conv_buf8, w["conv_w"], w["a_log"], w["dt_bias"], w["a_norm_g"], s_a0)
    oc, s_ct = _gla(hbig, hgate, w["c_w2p"], w["c_b2p"], w["c_norm_g"], s_c0t)

    zero_c = jnp.zeros((n, 1, LANES), F32)
    if past_b is None:
        lf, c = _fgate(hgate, 2, w["f_bias"], zero_c, True, 256)
        qp, kp, vp, k_cache, v_cache = _fox_prep(hbig, c, layer, depth, kv_caches)
        kv_caches = (k_cache, v_cache)
        ob = _fox_prompt(qp, kp, vp)
        kb = vb = None
    else:
        k_past, v_past, lf_past = past_b
        p = lf_past.shape[1]
        lf_past = jnp.pad(lf_past.astype(F32), ((0, 0), (0, 0), (0, LANES - B_HEADS)))
        _, c_past = _fgate(lf_past, 0, w["f_bias"], zero_c, False, 256)
        lf, c = _fgate(hgate, 2, w["f_bias"], c_past[:, p - 1:p, :], True, t)
        cq = jnp.swapaxes(c[:, :, :B_HEADS], 1, 2)[..., None]
        ckn = jnp.swapaxes(c[:, :, :B_HEADS], 1, 2)[:, :, None, :]
        ckp = jnp.swapaxes(c_past[:, :, :B_HEADS], 1, 2)[:, :, None, :]
        ob = _fox_sample(hbig, k_past, v_past, layer, cq, ckp, ckn)

    y = _outproj(oa.reshape(m, A_VAL), ob.reshape(m, B_WIDTH), oc.reshape(m, C_VAL),
                 w["w_out"], layer, x, alpha, 1024, 1024)

    conv_new = hbig[:, t - (CONV_W - 1):, COL_QKV_A:COL_QKV_A + A_CONV_DIM]
    if past_b is not None:
        kb = hbig[:, :, COL_K_B:COL_K_B + B_WIDTH].reshape(n, t, B_HEADS, HEAD_DIM)
        vb = hbig[:, :, COL_V_B:COL_V_B + B_WIDTH].reshape(n, t, B_HEADS, HEAD_DIM)
    s_c = jnp.swapaxes(s_ct, 2, 3)[:, :, :C_DK, :]
    return y, (conv_new, s_a, kb, vb, lf[:, :, :B_HEADS], s_c), kv_caches


def _layer(x, xb, n, t, conv_buf8, s_a0, past_b, s_c0t, w, layer, depth, alpha, kv_caches=None):
    (v1, s1, s2), st, kv_caches = _mixer(x, xb, n, t, conv_buf8, s_a0, past_b, s_c0t, w, layer, depth,
                                         alpha, kv_caches)
    x1b, = _ln_apply(v1, s1, s2, w["ln1_g"], w["ln1_b"], False)
    hmid = _swiglu(x1b, w["w_gate"], w["w_up"], layer, 1024, 256)
    v2, t1, t2 = _down_proj(hmid, w["w_down"], layer, v1, s1, s2, w["ln1_g"], w["ln1_b"], alpha, 512, 512)
    xb, x = _ln_apply(v2, t1, t2, w["ln2_g"], w["ln2_b"], True)
    return x, xb, st, kv_caches


def kernel(x_prompt, x_sample, state_a_conv, state_a_rec, cache_b_k, cache_b_v, cache_b_logf, state_c_rec, w_in, conv_w, a_log, dt_bias, a_norm_g, f_bias, c_w2, c_b2, c_norm_g, w_out, ln1_g, ln1_b, w_gate, w_up, w_down, ln2_g, ln2_b):
    depth = w_in.shape[0]
    alpha = (2 * depth) ** DEPTH_ALPHA_POW
    nb, tp, _ = x_prompt.shape
    ns, ts, _ = x_sample.shape

    hp = x_prompt.reshape(nb * tp, D_MODEL).astype(F32)
    hs = x_sample.reshape(ns * ts, D_MODEL).astype(F32)
    hpb = hp.astype(BF16)
    hsb = hs.astype(BF16)
    p_states, s_states = [], []
    w_big, w_gate4 = _prep_w_in(w_in)
    w_out_b, w_gate_b, w_up_b, w_down_b = (a.astype(BF16) for a in (w_out, w_gate, w_up, w_down))
    past_len = cache_b_k.shape[2]
    cache_k = cache_b_k.reshape(depth, ns, past_len * B_HEADS, HEAD_DIM).astype(F32)
    cache_v = cache_b_v.reshape(depth, ns, past_len * B_HEADS, HEAD_DIM).astype(F32)
    prompt_kv = None
    for l in range(depth):
        w = {
            "w_big": w_big, "w_gate4": w_gate4,
            "conv_w": conv_w[l].astype(F32),
            "a_log": _pad_vec(a_log[l]), "dt_bias": _pad_vec(dt_bias[l]),
            "a_norm_g": a_norm_g[l].reshape(1, HEAD_DIM).astype(F32),
            "f_bias": _pad_vec(f_bias[l]),
            "c_w2p": jnp.pad(_pad_heads_c(c_w2[l]), ((0, LANES - C_RANK), (0, 0))).astype(BF16),
            "c_b2p": _pad_heads_c(c_b2[l].reshape(1, C_KEY)).astype(F32),
            "c_norm_g": c_norm_g[l].reshape(1, C_DV).astype(F32),
            "w_out": w_out_b,
            "ln1_g": ln1_g[l], "ln1_b": ln1_b[l], "ln2_g": ln2_g[l], "ln2_b": ln2_b[l],
            "w_gate": w_gate_b, "w_up": w_up_b, "w_down": w_down_b,
        }
        hp, hpb, stp, prompt_kv = _layer(
            hp, hpb, nb, tp,
            jnp.zeros((nb, SUBLANES, A_CONV_DIM), F32),
            jnp.zeros((nb, A_HEADS, HEAD_DIM, HEAD_DIM), F32),
            None,
            jnp.zeros((nb, C_HEADS, C_DV, C_DK_PAD), F32), w, l, depth, alpha, prompt_kv)
        p_states.append(stp)
        buf8 = jnp.pad(state_a_conv[l].astype(F32), ((0, 0), (SUBLANES - (CONV_W - 1), 0), (0, 0)))
        s_c0t = jnp.pad(jnp.swapaxes(state_c_rec[l].astype(F32), 2, 3),
                        ((0, 0), (0, 0), (0, 0), (0, C_DK_PAD - C_DK)))
        hs, hsb, sts, _ = _layer(
            hs, hsb, ns, ts, buf8, state_a_rec[l].astype(F32),
            (cache_k, cache_v, cache_b_logf[l]), s_c0t, w, l, depth, alpha)
        s_states.append(sts)

    dp, ds = x_prompt.dtype, x_sample.dtype
    stack = lambda states, i, dt: jnp.stack([s[i] for s in states], axis=0).astype(dt)
    prompt_k, prompt_v = (a.reshape(depth, nb, tp, B_HEADS, HEAD_DIM).astype(dp) for a in prompt_kv)
    return ((hp.reshape(nb, tp, D_MODEL).astype(dp), hs.reshape(ns, ts, D_MODEL).astype(ds))
            + (stack(p_states, 0, dp), stack(p_states, 1, dp), prompt_k, prompt_v,
               stack(p_states, 4, dp), stack(p_states, 5, dp))
            + tuple(stack(s_states, i, ds) for i in range(6)))
```

```python
import functools

import jax
import jax.numpy as jnp
from jax import lax
from jax.experimental import pallas as pl
from jax.experimental.pallas import tpu as pltpu

F32 = jnp.float32
BF16 = jnp.bfloat16

D_MODEL = 4096
GDN_CHUNK = 64
GLA_CHUNK = 128
HEAD_DIM = 128
A_HEADS = 12
A_KEY = A_HEADS * HEAD_DIM
A_VAL = A_HEADS * HEAD_DIM
A_CONV_DIM = 2 * A_KEY + A_VAL
CONV_W = 4
B_HEADS = 8
B_WIDTH = B_HEADS * HEAD_DIM
C_HEADS = 4
C_DV = 384
C_DK = 192
C_DK_PAD = 256
C_KEY = C_HEADS * C_DK
C_KEY_PAD = C_HEADS * C_DK_PAD
C_VAL = C_HEADS * C_DV
C_RANK = 16
C_TAU = 16.0
D_FF = 11008
DEPTH_ALPHA_POW = 0.25
LN_EPS = 1e-5
RMS_EPS = 1e-6

LANES = 128
SUBLANES = 8
VMEM_LIMIT = 56 * 1024 * 1024

COL_QKV_A = 0
COL_Z_A = 4608
COL_V_C = 6144
COL_R_C = 7680
COL_Q_B = 9216
COL_K_B = 10240
COL_V_B = 11264
COL_Q_C = 12288
COL_K_C = 13312
IN_BIG = 14336
IN_GATE = LANES
LANE_A_A = 0
LANE_B_A = LANE_A_A + A_HEADS
LANE_F_B = LANE_B_A + A_HEADS
LANE_LR_C = LANE_F_B + B_HEADS

NEG_BIG = -1e30
LOG2E = 1.4426950408889634
FOX_HEADS_PER_STEP = 4
GDN_SEQS_PER_STEP = 1
GDN_HEADS_PER_GROUP = 3


def _cparams(sem):
    return pltpu.CompilerParams(dimension_semantics=sem, vmem_limit_bytes=VMEM_LIMIT)


def _sigmoid(x):
    return 1.0 / (1.0 + jnp.exp(-x))


def _silu(x):
    return x * _sigmoid(x)


def _softplus(x):
    return jnp.maximum(x, 0.0) + jnp.log(1.0 + jnp.exp(-jnp.abs(x)))


def _log_sigmoid(x):
    return -_softplus(-x)


def _split3(x):
    x1 = x.astype(BF16)
    r1 = x - x1.astype(F32)
    x2 = r1.astype(BF16)
    x3 = (r1 - x2.astype(F32)).astype(BF16)
    return x1, x2, x3


def _dot(a, b):
    return jnp.dot(a, b, preferred_element_type=F32)


def _dot_nt(a, b):
    return lax.dot_general(a, b, (((1,), (1,)), ((), ())), preferred_element_type=F32)


def _dot_tn(a, b):
    return lax.dot_general(a, b, (((0,), (0,)), ((), ())), preferred_element_type=F32)


def _dot_exact_lhs(a_bf16, x):
    x1, x2, x3 = _split3(x)
    return _dot(a_bf16, x1) + _dot(a_bf16, x2) + _dot(a_bf16, x3)


def _solve_dot(a, b):
    return _dot(a.astype(BF16), b.astype(BF16))


def _iota2(shape, dim):
    return lax.broadcasted_iota(jnp.int32, shape, dim)


def _tri_incl(n):
    return (_iota2((n, n), 1) <= _iota2((n, n), 0)).astype(BF16)


def _mm_kernel(x_ref, w_ref, o_ref):
    o_ref[...] = _dot(x_ref[...], w_ref[...]).astype(o_ref.dtype)


def _matmul(x, w, layer, tm, tn, out_dtype):
    m, k = x.shape
    n = w.shape[2]
    tm = min(tm, m)
    return pl.pallas_call(
        _mm_kernel,
        grid=(m // tm, n // tn),
        in_specs=[pl.BlockSpec((tm, k), lambda i, j: (i, 0)),
                  pl.BlockSpec((None, k, tn), lambda i, j: (layer, 0, j))],
        out_specs=pl.BlockSpec((tm, tn), lambda i, j: (i, j)),
        out_shape=jax.ShapeDtypeStruct((m, n), out_dtype),
        compiler_params=_cparams(("parallel", "parallel")),
    )(x, w)


def _lane_partial(v):
    part = v[:, 0:LANES]
    for c in range(1, v.shape[1] // LANES):
        part = part + v[:, c * LANES:(c + 1) * LANES]
    return part


def _residual_epilogue(alpha, r, acc, v_ref, s1_ref, s2_ref):
    v = alpha * r + acc
    v_ref[...] = v
    p1 = _lane_partial(v)
    p2 = _lane_partial(v * v)
    first = pl.program_id(1) == 0
    s1_ref[...] = jnp.where(first, p1, s1_ref[...] + p1)
    s2_ref[...] = jnp.where(first, p2, s2_ref[...] + p2)


def _row_stats(s1, s2, width):
    mu = jnp.sum(s1, axis=-1, keepdims=True) * (1.0 / width)
    var = jnp.sum(s2, axis=-1, keepdims=True) * (1.0 / width) - mu * mu
    return mu, lax.rsqrt(var + LN_EPS)


def _outproj_kernel(alpha, a_ref, b_ref, c_ref, w_ref, r_ref, v_ref, s1_ref, s2_ref):
    acc = (_dot(a_ref[...], w_ref[0:A_VAL, :])
           + _dot(b_ref[...], w_ref[A_VAL:A_VAL + B_WIDTH, :])
           + _dot(c_ref[...], w_ref[A_VAL + B_WIDTH:D_MODEL, :]))
    _residual_epilogue(alpha, r_ref[...], acc, v_ref, s1_ref, s2_ref)


def _resid_out(m, tm, tn):
    tile = pl.BlockSpec((tm, tn), lambda i, j: (i, j))
    stat = pl.BlockSpec((tm, LANES), lambda i, j: (i, 0))
    return ([tile, stat, stat],
            [jax.ShapeDtypeStruct((m, D_MODEL), F32)] + [jax.ShapeDtypeStruct((m, LANES), F32)] * 2)


def _outproj(oa, ob, oc, w, layer, resid, alpha, tm, tn):
    m = oa.shape[0]
    tm = min(tm, m)
    rows = lambda width: pl.BlockSpec((tm, width), lambda i, j: (i, 0))
    out_specs, out_shape = _resid_out(m, tm, tn)
    return pl.pallas_call(
        functools.partial(_outproj_kernel, alpha),
        grid=(m // tm, D_MODEL // tn),
        in_specs=[rows(A_VAL), rows(B_WIDTH), rows(C_VAL),
                  pl.BlockSpec((None, D_MODEL, tn), lambda i, j: (layer, 0, j)),
                  pl.BlockSpec((tm, tn), lambda i, j: (i, j))],
        out_specs=out_specs, out_shape=out_shape,
        compiler_params=_cparams(("parallel", "arbitrary")),
    )(oa, ob, oc, w, resid)


def _down_kernel(alpha, h_ref, w_ref, v1_ref, s1_ref, s2_ref, g_ref, b_ref, v_ref, t1_ref, t2_ref):
    acc = _dot(h_ref[...], w_ref[...])
    mu, rstd = _row_stats(s1_ref[...], s2_ref[...], D_MODEL)
    x1 = (v1_ref[...] - mu) * rstd * g_ref[...] + b_ref[...]
    _residual_epilogue(alpha, x1, acc, v_ref, t1_ref, t2_ref)


def _down_proj(hmid, w, layer, v1, s1, s2, g, b, alpha, tm, tn):
    m, k = hmid.shape
    tm = min(tm, m)
    tile = pl.BlockSpec((tm, tn), lambda i, j: (i, j))
    stat = pl.BlockSpec((tm, LANES), lambda i, j: (i, 0))
    vec = pl.BlockSpec((1, tn), lambda i, j: (0, j))
    out_specs, out_shape = _resid_out(m, tm, tn)
    return pl.pallas_call(
        functools.partial(_down_kernel, alpha),
        grid=(m // tm, D_MODEL // tn),
        in_specs=[pl.BlockSpec((tm, k), lambda i, j: (i, 0)),
                  pl.BlockSpec((None, k, tn), lambda i, j: (layer, 0, j)),
                  tile, stat, stat, vec, vec],
        out_specs=out_specs, out_shape=out_shape,
        compiler_params=_cparams(("parallel", "arbitrary")),
    )(hmid, w, v1, s1, s2, g.reshape(1, D_MODEL), b.reshape(1, D_MODEL))


def _swiglu_kernel(x_ref, wg_ref, wu_ref, o_ref):
    x = x_ref[...]
    a = _dot(x, wg_ref[...])
    b = _dot(x, wu_ref[...])
    o_ref[...] = (_silu(a) * b).astype(o_ref.dtype)


def _swiglu(x, wg, wu, layer, tm, tn):
    m, k = x.shape
    n = wg.shape[2]
    tm = min(tm, m)
    wspec = pl.BlockSpec((None, k, tn), lambda i, j: (layer, 0, j))
    return pl.pallas_call(
        _swiglu_kernel,
        grid=(m // tm, n // tn),
        in_specs=[pl.BlockSpec((tm, k), lambda i, j: (i, 0)), wspec, wspec],
        out_specs=pl.BlockSpec((tm, tn), lambda i, j: (i, j)),
        out_shape=jax.ShapeDtypeStruct((m, n), BF16),
        compiler_params=_cparams(("parallel", "parallel")),
    )(x, wg, wu)


def _ln_apply_kernel(with_f32, v_ref, s1_ref, s2_ref, g_ref, b_ref, *out_refs):
    mu, rstd = _row_stats(s1_ref[...], s2_ref[...], D_MODEL)
    out = (v_ref[...] - mu) * rstd * g_ref[...] + b_ref[...]
    out_refs[0][...] = out.astype(BF16)
    if with_f32:
        out_refs[1][...] = out


def _ln_apply(v, s1, s2, g, b, with_f32, tm=256):
    m, d = v.shape
    tm = min(tm, m)
    row = pl.BlockSpec((tm, d), lambda i: (i, 0))
    stat = pl.BlockSpec((tm, LANES), lambda i: (i, 0))
    vec = pl.BlockSpec((1, d), lambda i: (0, 0))
    n_out = 2 if with_f32 else 1
    return pl.pallas_call(
        functools.partial(_ln_apply_kernel, with_f32),
        grid=(m // tm,),
        in_specs=[row, stat, stat, vec, vec],
        out_specs=[row] * n_out,
        out_shape=[jax.ShapeDtypeStruct((m, d), BF16), jax.ShapeDtypeStruct((m, d), F32)][:n_out],
        compiler_params=_cparams(("parallel",)),
    )(v, s1, s2, g.reshape(1, d), b.reshape(1, d))


def _fgate_kernel(apply_gate, tb, x_ref, bias_ref, c0_ref, lf_ref, c_ref, carry):
    @pl.when(pl.program_id(1) == 0)
    def _():
        carry[...] = c0_ref[0]

    x = x_ref[0]
    lf = _log_sigmoid(x + bias_ref[...]) if apply_gate else x
    c = _dot_exact_lhs(_tri_incl(tb), lf) + carry[...]
    lf_ref[0] = lf
    c_ref[0] = c
    carry[...] = c[tb - 1:tb, :]


def _fgate(x, bias, c0, apply_gate, tb):
    n, t, _ = x.shape
    tb = min(tb, t)
    blk = pl.BlockSpec((1, tb, LANES), lambda i, j: (i, j, 0))
    return pl.pallas_call(
        functools.partial(_fgate_kernel, apply_gate, tb),
        grid=(n, t // tb),
        in_specs=[pl.BlockSpec((1, tb, LANES), lambda i, j: (i, j, 0)),
                  pl.BlockSpec((1, LANES), lambda i, j: (0, 0)),
                  pl.BlockSpec((1, 1, LANES), lambda i, j: (i, 0, 0))],
        out_specs=[blk, blk],
        out_shape=[jax.ShapeDtypeStruct((n, t, LANES), F32)] * 2,
        scratch_shapes=[pltpu.VMEM((1, LANES), F32)],
        compiler_params=_cparams(("parallel", "arbitrary")),
    )(x, bias, c0)


def _gdn_kernel(L, NB, qkv_ref, z_ref, gate_ref, buf_ref, convw_ref, alog_ref, dtb_ref, ng_ref, s0_ref,
                o_ref, s_ref, xs):
    @pl.when(pl.program_id(1) == 0)
    def _():
        xs[:, 0:SUBLANES, :] = buf_ref[...]
        s_ref[...] = s0_ref[...]

    xs[:, SUBLANES:SUBLANES + L, :] = qkv_ref[...]

    def conv_cols(b, c0):
        acc = xs[b, 5:5 + L, c0:c0 + HEAD_DIM] * convw_ref[0:1, c0:c0 + HEAD_DIM]
        for i in range(1, CONV_W):
            acc = acc + xs[b, 5 + i:5 + i + L, c0:c0 + HEAD_DIM] * convw_ref[i:i + 1, c0:c0 + HEAD_DIM]
        return _silu(acc)

    def l2n(x):
        return x * lax.rsqrt(jnp.sum(x * x, axis=-1, keepdims=True) + RMS_EPS)

    tri = _tri_incl(L)
    eye_l = (_iota2((LANES, LANES), 0) == _iota2((LANES, LANES), 1)).astype(BF16)
    seqs = []
    for b in range(NB):
        a_in = gate_ref[b]
        g = -jnp.exp(alog_ref[...]) * _softplus(a_in + dtb_ref[...])
        gc = _dot_exact_lhs(tri, g)
        g1, g2, g3 = _split3(gc)
        gct = _dot_nt(eye_l, g1) + _dot_nt(eye_l, g2) + _dot_nt(eye_l, g3)
        seqs.append(dict(gc=gc, gct=gct, beta=_sigmoid(a_in)))

    row = _iota2((L, L), 0)
    col = _iota2((L, L), 1)
    eye = (row == col).astype(F32)
    n_lvl = L.bit_length() - 2

    mm = _solve_dot

    def st_prep(hd):
        b, h = hd["b"], hd["h"]
        sq = seqs[b]
        q = l2n(conv_cols(b, h * HEAD_DIM)) * (HEAD_DIM ** -0.5)
        k = l2n(conv_cols(b, A_KEY + h * HEAD_DIM))
        v = conv_cols(b, 2 * A_KEY + h * HEAD_DIM)
        gcol = sq["gc"][:, h:h + 1]
        bcol = sq["beta"][:, LANE_B_A + h:LANE_B_A + h + 1]
        glast = sq["gc"][L - 1:L, h:h + 1]
        dec = jnp.exp(jnp.where(row >= col, gcol - sq["gct"][h:h + 1, :], NEG_BIG))
        eg = jnp.exp(gcol)
        qb = q.astype(BF16)
        kb = k.astype(BF16)
        p = -(bcol * _dot_nt(kb, kb) * jnp.where(row > col, dec, 0.0))
        hd.update(qb=qb, p=p, t=eye + p, eg=eg, e_last=jnp.exp(glast),
                  qkd=(_dot_nt(qb, kb) * dec).astype(BF16),
                  rhs=jnp.concatenate([(bcol * eg) * k, bcol * v], axis=1),
                  k_end=(k * jnp.exp(glast - gcol)).astype(BF16),
                  s=s_ref[b, h])

    def st_square(hd):
        hd["p"] = mm(hd["p"], hd["p"])

    def st_level(hd):
        pt = mm(hd["p"], jnp.concatenate([hd["p"], hd["t"]], axis=1))
        hd["t"] = hd["t"] + pt[:, L:2 * L]
        hd["p"] = pt[:, 0:L]

    def st_last_level(hd):
        hd["t"] = hd["t"] + mm(hd["p"], hd["t"])

    def st_solve(hd):
        hd["tr"] = mm(hd["t"], hd["rhs"])

    def st_state(hd):
        tr = hd["tr"]
        sb = hd["s"].astype(BF16)
        hd["ub"] = (tr[:, HEAD_DIM:2 * HEAD_DIM] - _dot(tr[:, 0:HEAD_DIM].astype(BF16), sb)).astype(BF16)
        hd["qs"] = _dot(hd["qb"], sb)

    def st_out(hd):
        o = hd["eg"] * hd["qs"] + _dot(hd["qkd"], hd["ub"])
        hd["s_new"] = hd["e_last"] * hd["s"] + _dot_tn(hd["k_end"], hd["ub"])
        hd["o"] = o * lax.rsqrt(jnp.mean(o * o, axis=-1, keepdims=True) + RMS_EPS) * ng_ref[...]

    stages = [st_prep, st_square] + [st_level] * (n_lvl - 1) + [st_last_level, st_solve, st_state, st_out]
    heads = [dict(b=b, h=h) for b in range(NB) for h in range(A_HEADS)]
    groups = [heads[i:i + GDN_HEADS_PER_GROUP] for i in range(0, len(heads), GDN_HEADS_PER_GROUP)]
    for wave in range(len(groups) + len(stages) - 1):
        for g, group in enumerate(groups):
            if 0 <= wave - g < len(stages):
                for hd in group:
                    stages[wave - g](hd)

    for hd in heads:
        b, h = hd["b"], hd["h"]
        s_ref[b, h] = hd["s_new"]
        zh = z_ref[b, :, h * HEAD_DIM:(h + 1) * HEAD_DIM]
        o_ref[b, :, h * HEAD_DIM:(h + 1) * HEAD_DIM] = (hd["o"] * _silu(zh)).astype(BF16)

    tail = xs[:, L:L + SUBLANES, :]
    xs[:, 0:SUBLANES, :] = tail


def _gdn(hbig, hgate, conv_buf8, conv_w, a_log, dt_bias, norm_g, s0):
    n, t, _ = hbig.shape
    L = min(t, GDN_CHUNK)
    nb = GDN_SEQS_PER_STEP
    const2 = lambda i, j: (0, 0)
    state = pl.BlockSpec((nb, A_HEADS, HEAD_DIM, HEAD_DIM), lambda i, j: (i, 0, 0, 0))
    return pl.pallas_call(
        functools.partial(_gdn_kernel, L, nb),
        grid=(n // nb, t // L),
        in_specs=[pl.BlockSpec((nb, L, A_CONV_DIM), lambda i, j: (i, j, COL_QKV_A // A_CONV_DIM)),
                  pl.BlockSpec((nb, L, A_VAL), lambda i, j: (i, j, COL_Z_A // A_VAL)),
                  pl.BlockSpec((nb, L, LANES), lambda i, j: (i, j, 0)),
                  pl.BlockSpec((nb, SUBLANES, A_CONV_DIM), lambda i, j: (i, 0, 0)),
                  pl.BlockSpec((CONV_W, A_CONV_DIM), const2),
                  pl.BlockSpec((1, LANES), const2),
                  pl.BlockSpec((1, LANES), const2),
                  pl.BlockSpec((1, HEAD_DIM), const2),
                  state],
        out_specs=[pl.BlockSpec((nb, L, A_VAL), lambda i, j: (i, j, 0)), state],
        out_shape=[jax.ShapeDtypeStruct((n, t, A_VAL), BF16),
                   jax.ShapeDtypeStruct((n, A_HEADS, HEAD_DIM, HEAD_DIM), F32)],
        scratch_shapes=[pltpu.VMEM((nb, L + SUBLANES, A_CONV_DIM), F32)],
        compiler_params=_cparams(("parallel", "arbitrary")),
    )(hbig, hbig, hgate, conv_buf8, conv_w, a_log, dt_bias, norm_g, s0)


def _gla_kernel(L, q_ref, k_ref, v_ref, r_ref, gate_ref, w2_ref, b2_ref, ng_ref, s0_ref,
                o_ref, s_ref):
    @pl.when(pl.program_id(1) == 0)
    def _():
        s_ref[0] = s0_ref[0]

    z = _dot(gate_ref[0].astype(BF16), w2_ref[...]) + b2_ref[...]
    lg = _log_sigmoid(z) * (1.0 / C_TAU)
    b = _dot_exact_lhs(_tri_incl(L), lg)
    causal = _iota2((L, L), 0) >= _iota2((L, L), 1)

    heads = []
    for h in range(C_HEADS):
        ks = slice(h * C_DK_PAD, (h + 1) * C_DK_PAD)
        vs = slice(h * C_DV, (h + 1) * C_DV)
        bh = b[:, ks]
        blast = bh[L - 1:L, :]
        bref = bh[L // 2:L // 2 + 1, :]
        q = q_ref[0, :, ks] * (C_DK ** -0.5)
        k = k_ref[0, :, ks]
        heads.append(dict(
            vs=vs, vb=v_ref[0, :, vs].astype(BF16), st=s_ref[0, h],
            q_state=(q * jnp.exp(bh)).astype(BF16),
            q_in=(q * jnp.exp(bh - bref)).astype(BF16),
            k_in=(k * jnp.exp(bref - bh)).astype(BF16),
            k_end=(k * jnp.exp(blast - bh)).astype(BF16),
            e_last=jnp.exp(blast)))
    for hd in heads:
        hd["att"] = jnp.where(causal, _dot_nt(hd["q_in"], hd["k_in"]), 0.0).astype(BF16)
        hd["o"] = _dot_nt(hd["q_state"], hd["st"].astype(BF16))
        hd["st_new"] = hd["st"] * hd["e_last"] + _dot_tn(hd["vb"], hd["k_end"])
    for hd in heads:
        o = hd["o"] + _dot(hd["att"], hd["vb"])
        hd["o"] = o * lax.rsqrt(jnp.mean(o * o, axis=-1, keepdims=True) + RMS_EPS) * ng_ref[...]
    for h, hd in enumerate(heads):
        s_ref[0, h] = hd["st_new"]
        o_ref[0, :, hd["vs"]] = (hd["o"] * _silu(r_ref[0, :, hd["vs"]])).astype(BF16)


def _gla(hbig, hgate, w2p, b2p, norm_g, s0t):
    n, t, _ = hbig.shape
    L = min(t, GLA_CHUNK)
    const2 = lambda i, j: (0, 0)
    state = pl.BlockSpec((1, C_HEADS, C_DV, C_DK_PAD), lambda i, j: (i, 0, 0, 0))
    return pl.pallas_call(
        functools.partial(_gla_kernel, L),
        grid=(n, t // L),
        in_specs=[pl.BlockSpec((1, L, C_KEY_PAD), lambda i, j: (i, j, COL_Q_C // C_KEY_PAD)),
                  pl.BlockSpec((1, L, C_KEY_PAD), lambda i, j: (i, j, COL_K_C // C_KEY_PAD)),
                  pl.BlockSpec((1, L, C_VAL), lambda i, j: (i, j, COL_V_C // C_VAL)),
                  pl.BlockSpec((1, L, C_VAL), lambda i, j: (i, j, COL_R_C // C_VAL)),
                  pl.BlockSpec((1, L, LANES), lambda i, j: (i, j, 0)),
                  pl.BlockSpec((LANES, C_KEY_PAD), const2),
                  pl.BlockSpec((1, C_KEY_PAD), const2),
                  pl.BlockSpec((1, C_DV), const2),
                  state],
        out_specs=[pl.BlockSpec((1, L, C_VAL), lambda i, j: (i, j, 0)), state],
        out_shape=[jax.ShapeDtypeStruct((n, t, C_VAL), BF16),
                   jax.ShapeDtypeStruct((n, C_HEADS, C_DV, C_DK_PAD), F32)],
        compiler_params=_cparams(("parallel", "arbitrary")),
    )(hbig, hbig, hbig, hbig, hgate, w2p, b2p, norm_g, s0t)


def _fox_prep_kernel(q_ref, k_ref, v_ref, c_ref, qp_ref, kp_ref, vp_ref):
    tb = q_ref.shape[1]
    lane = _iota2((tb, LANES), 1)
    ones_q = jnp.where((lane >= 3) & (lane < 6), 1.0, 0.0)
    ones_k = jnp.where(lane < 3, 1.0, 0.0)
    c2 = c_ref[0] * LOG2E
    for h in range(B_HEADS):
        hs = slice(h * HEAD_DIM, (h + 1) * HEAD_DIM)
        c1, cm, cl = (t.astype(F32) for t in _split3(c2[:, LANE_F_B + h:LANE_F_B + h + 1]))
        ext_q = jnp.where(lane == 0, c1, jnp.where(lane == 1, cm, jnp.where(lane == 2, cl, ones_q)))
        ext_k = jnp.where(lane == 3, -c1, jnp.where(lane == 4, -cm, jnp.where(lane == 5, -cl, ones_k)))
        base = 2 * h * HEAD_DIM
        qp_ref[0, :, base:base + HEAD_DIM] = (q_ref[0, :, hs] * (HEAD_DIM ** -0.5 * LOG2E)).astype(BF16)
        qp_ref[0, :, base + HEAD_DIM:base + 2 * HEAD_DIM] = ext_q.astype(BF16)
        kp_ref[0, :, base:base + HEAD_DIM] = k_ref[0, :, hs].astype(BF16)
        kp_ref[0, :, base + HEAD_DIM:base + 2 * HEAD_DIM] = ext_k.astype(BF16)
    vp_ref[0] = v_ref[0].astype(BF16)


def _fox_prep_cache_kernel(q_ref, k_ref, v_ref, c_ref, *rest):
    qp_ref, kp_ref, vp_ref, kc_ref, vc_ref = rest[-5:]
    _fox_prep_kernel(q_ref, k_ref, v_ref, c_ref, qp_ref, kp_ref, vp_ref)
    kc_ref[0] = k_ref[0]
    vc_ref[0] = v_ref[0]


def _fox_prep(hbig, c, layer, depth, caches, tb=256):
    n, t, _ = hbig.shape
    tb = min(tb, t)
    col = lambda blk: (lambda i, j: (i, j, blk))
    wide = pl.BlockSpec((1, tb, 2 * B_WIDTH), col(0))
    cache_spec = pl.BlockSpec((None, 1, tb, B_WIDTH), lambda i, j: (layer, i, j, 0))
    cache_shape = jax.ShapeDtypeStruct((depth, n, t, B_WIDTH), F32)
    in_specs = [pl.BlockSpec((1, tb, B_WIDTH), col(COL_Q_B // B_WIDTH)),
                pl.BlockSpec((1, tb, B_WIDTH), col(COL_K_B // B_WIDTH)),
                pl.BlockSpec((1, tb, B_WIDTH), col(COL_V_B // B_WIDTH)),
                pl.BlockSpec((1, tb, LANES), col(0))]
    args = [hbig, hbig, hbig, c]
    aliases = {}
    if caches is not None:
        in_specs += [pl.BlockSpec(memory_space=pl.ANY)] * 2
        args += list(caches)
        aliases = {4: 3, 5: 4}
    return pl.pallas_call(
        _fox_prep_cache_kernel,
        grid=(n, t // tb),
        in_specs=in_specs,
        out_specs=[wide, wide, pl.BlockSpec((1, tb, B_WIDTH), col(0)), cache_spec, cache_spec],
        out_shape=[jax.ShapeDtypeStruct((n, t, 2 * B_WIDTH), BF16),
                   jax.ShapeDtypeStruct((n, t, 2 * B_WIDTH), BF16),
                   jax.ShapeDtypeStruct((n, t, B_WIDTH), BF16),
                   cache_shape, cache_shape],
        input_output_aliases=aliases,
        compiler_params=_cparams(("parallel", "parallel")),
    )(*args)


def _fox_prompt_kernel(tq, q_ref, k_ref, v_ref, o_ref, m_sc, l_sc, acc_sc):
    qi = pl.program_id(2)
    m_sc[...] = jnp.full(m_sc.shape, NEG_BIG, F32)
    l_sc[...] = jnp.zeros(l_sc.shape, F32)
    acc_sc[...] = jnp.zeros(acc_sc.shape, F32)
    n_tiles = tq // LANES
    heads = range(FOX_HEADS_PER_STEP)
    qw = 2 * HEAD_DIM

    def block(j, masked):
        off = pl.multiple_of(j * tq, tq)
        ss = [_dot_nt(q_ref[0, :, h * qw:(h + 1) * qw], k_ref[0, pl.ds(off, tq), h * qw:(h + 1) * qw])
              for h in heads]
        if masked:
            visible = _iota2((tq, tq), 1) <= _iota2((tq, tq), 0)
            ss = [jnp.where(visible, s, NEG_BIG) for s in ss]
        new = []
        for h, s in zip(heads, ss):
            tiles = [s[:, c * LANES:(c + 1) * LANES] for c in range(n_tiles)]
            mx = tiles[0]
            for tl in tiles[1:]:
                mx = jnp.maximum(mx, tl)
            m_prev = m_sc[h]
            m_new = jnp.maximum(m_prev, jnp.max(mx, axis=-1, keepdims=True))
            alpha = jnp.exp2(m_prev - m_new)
            ps = [jnp.exp2(tl - m_new) for tl in tiles]
            psum = ps[0]
            for pt in ps[1:]:
                psum = psum + pt
            p = jnp.concatenate([pt.astype(BF16) for pt in ps], axis=1)
            pv = _dot(p, v_ref[0, pl.ds(off, tq), h * HEAD_DIM:(h + 1) * HEAD_DIM])
            new.append((m_new, alpha * l_sc[h] + psum, alpha * acc_sc[h] + pv))
        for h, (m_new, l_new, acc_new) in zip(heads, new):
            m_sc[h] = m_new
            l_sc[h] = l_new
            acc_sc[h] = acc_new

    def body(j, carry):
        block(j, False)
        return carry

    lax.fori_loop(0, qi, body, 0)
    block(qi, True)
    for h in heads:
        o_ref[0, :, h * HEAD_DIM:(h + 1) * HEAD_DIM] = (
            acc_sc[h] / jnp.sum(l_sc[h], axis=-1, keepdims=True)).astype(BF16)


def _fox_prompt(qp, kp, vp, tq=512):
    n, t, _ = vp.shape
    tq = min(tq, t)
    hps = FOX_HEADS_PER_STEP
    return pl.pallas_call(
        functools.partial(_fox_prompt_kernel, tq),
        grid=(n, B_HEADS // hps, t // tq),
        in_specs=[pl.BlockSpec((1, tq, hps * 2 * HEAD_DIM), lambda i, h, qi: (i, qi, h)),
                  pl.BlockSpec((1, t, hps * 2 * HEAD_DIM), lambda i, h, qi: (i, 0, h),
                               pipeline_mode=pl.Buffered(1)),
                  pl.BlockSpec((1, t, hps * HEAD_DIM), lambda i, h, qi: (i, 0, h),
                               pipeline_mode=pl.Buffered(1))],
        out_specs=pl.BlockSpec((1, tq, hps * HEAD_DIM), lambda i, h, qi: (i, qi, h)),
        out_shape=jax.ShapeDtypeStruct((n, t, B_WIDTH), BF16),
        scratch_shapes=[pltpu.VMEM((hps, tq, LANES), F32),
                        pltpu.VMEM((hps, tq, LANES), F32),
                        pltpu.VMEM((hps, tq, HEAD_DIM), F32)],
        compiler_params=_cparams(("parallel", "parallel", "arbitrary")),
    )(qp, kp, vp)


def _fox_sample_kernel(t, p, q_ref, kn_ref, vn_ref, kp_ref, vp_ref, cq_ref, ckp_ref, ckn_ref, o_ref):
    causal = _iota2((t, t), 1) <= _iota2((t, t), 0)
    for h in range(B_HEADS):
        hs = slice(h * HEAD_DIM, (h + 1) * HEAD_DIM)
        kp = kp_ref[0, pl.ds(h, p, stride=B_HEADS), :].astype(BF16)
        vp = vp_ref[0, pl.ds(h, p, stride=B_HEADS), :].astype(BF16)
        qb = (q_ref[0, :, hs] * (HEAD_DIM ** -0.5)).astype(BF16)
        cq = cq_ref[0, h]
        sp = _dot_nt(qb, kp) + (cq - ckp_ref[0, h])
        sn = _dot_nt(qb, kn_ref[0, :, hs].astype(BF16)) + (cq - ckn_ref[0, h])
        sn = jnp.where(causal, sn, NEG_BIG)
        m = jnp.maximum(jnp.max(sp, axis=-1, keepdims=True), jnp.max(sn, axis=-1, keepdims=True))
        pp = jnp.exp(sp - m)
        pn = jnp.exp(sn - m)
        den = jnp.sum(pp, axis=-1, keepdims=True) + jnp.sum(pn, axis=-1, keepdims=True)
        o = _dot(pp.astype(BF16), vp) + _dot(pn.astype(BF16), vn_ref[0, :, hs].astype(BF16))
        o_ref[0, :, hs] = (o / den).astype(BF16)


def _fox_sample(hbig, k_past, v_past, layer, cq, ckp, ckn):
    n, t, _ = hbig.shape
    rows = k_past.shape[2]
    p = rows // B_HEADS
    col = lambda base: (lambda i: (i, 0, base // B_WIDTH))
    past = pl.BlockSpec((None, 1, rows, HEAD_DIM), lambda i: (layer, i, 0, 0))
    whole = lambda a: pl.BlockSpec((1,) + a.shape[1:], lambda i: (i, 0, 0, 0))
    return pl.pallas_call(
        functools.partial(_fox_sample_kernel, t, p),
        grid=(n,),
        in_specs=[pl.BlockSpec((1, t, B_WIDTH), col(COL_Q_B)),
                  pl.BlockSpec((1, t, B_WIDTH), col(COL_K_B)),
                  pl.BlockSpec((1, t, B_WIDTH), col(COL_V_B)),
                  past, past, whole(cq), whole(ckp), whole(ckn)],
        out_specs=pl.BlockSpec((1, t, B_WIDTH), lambda i: (i, 0, 0)),
        out_shape=jax.ShapeDtypeStruct((n, t, B_WIDTH), BF16),
        compiler_params=_cparams(("parallel",)),
    )(hbig, hbig, hbig, k_past, v_past, cq, ckp, ckn)


def _pad_cols(w, width):
    return jnp.pad(w, [(0, 0)] * (w.ndim - 1) + [(0, width - w.shape[-1])])


def _pad_heads_c(w):
    lead = w.shape[:-1]
    w = w.reshape(lead + (C_HEADS, C_DK))
    w = jnp.pad(w, [(0, 0)] * len(lead) + [(0, 0), (0, C_DK_PAD - C_DK)])
    return w.reshape(lead + (C_KEY_PAD,))


def _prep_w_in(w):
    o = 0
    parts = {}
    for name, size in (("qkv_a", A_CONV_DIM), ("z_a", A_VAL), ("a_a", A_HEADS), ("b_a", A_HEADS),
                       ("q_b", B_WIDTH), ("k_b", B_WIDTH), ("v_b", B_WIDTH), ("f_b", B_HEADS),
                       ("q_c", C_KEY), ("k_c", C_KEY), ("v_c", C_VAL), ("r_c", C_VAL), ("lr_c", C_RANK)):
        parts[name] = w[..., o:o + size].astype(BF16)
        o += size
    big = jnp.concatenate([parts["qkv_a"], parts["z_a"], parts["v_c"], parts["r_c"],
                           parts["q_b"], parts["k_b"], parts["v_b"],
                           _pad_heads_c(parts["q_c"]), _pad_heads_c(parts["k_c"])], axis=-1)
    gate = _pad_cols(jnp.concatenate([parts[nm] for nm in ("a_a", "b_a", "f_b", "lr_c")], axis=-1), LANES)
    return big, gate


def _pad_vec(v):
    return _pad_cols(v.reshape(1, -1).astype(F32), LANES)


def _mixer(x, xb, n, t, conv_buf8, s_a0, past_b, s_c0t, w, layer, depth, alpha, kv_caches):
    m = n * t
    hbig = _matmul(xb, w["w_big"], layer, 1024, 1024, F32).reshape(n, t, IN_BIG)
    hgate = _matmul(xb, w["w_gate4"], layer, 1024, IN_GATE, F32).reshape(n, t, IN_GATE)

    oa, s_a = _gdn(hbig, hgate, conv_buf8, w["conv_w"], w["a_log"], w["dt_bias"], w["a_norm_g"], s_a0)
    oc, s_ct = _gla(hbig, hgate, w["c_w2p"], w["c_b2p"], w["c_norm_g"], s_c0t)

    zero_c = jnp.zeros((n, 1, LANES), F32)
    if past_b is None:
        lf, c = _fgate(hgate, w["f_bias"], zero_c, True, 256)
        qp, kp, vp, k_cache, v_cache = _fox_prep(hbig, c, layer, depth, kv_caches)
        kv_caches = (k_cache, v_cache)
        ob = _fox_prompt(qp, kp, vp)
        kb = vb = None
    else:
        k_past, v_past, lf_past = past_b
        p = lf_past.shape[1]
        lf_past = jnp.pad(lf_past.astype(F32), ((0, 0), (0, 0), (LANE_F_B, LANES - LANE_F_B - B_HEADS)))
        _, c_past = _fgate(lf_past, w["f_bias"], zero_c, False, 256)
        lf, c = _fgate(hgate, w["f_bias"], c_past[:, p - 1:p, :], True, t)
        fb = slice(LANE_F_B, LANE_F_B + B_HEADS)
        cq = jnp.swapaxes(c[:, :, fb], 1, 2)[..., None]
        ckn = jnp.swapaxes(c[:, :, fb], 1, 2)[:, :, None, :]
        ckp = jnp.swapaxes(c_past[:, :, fb], 1, 2)[:, :, None, :]
        ob = _fox_sample(hbig, k_past, v_past, layer, cq, ckp, ckn)

    y = _outproj(oa.reshape(m, A_VAL), ob.reshape(m, B_WIDTH), oc.reshape(m, C_VAL),
                 w["w_out"], layer, x, alpha, 1024, 1024)

    conv_new = hbig[:, t - (CONV_W - 1):, COL_QKV_A:COL_QKV_A + A_CONV_DIM]
    if past_b is not None:
        kb = hbig[:, :, COL_K_B:COL_K_B + B_WIDTH].reshape(n, t, B_HEADS, HEAD_DIM)
        vb = hbig[:, :, COL_V_B:COL_V_B + B_WIDTH].reshape(n, t, B_HEADS, HEAD_DIM)
    s_c = jnp.swapaxes(s_ct, 2, 3)[:, :, :C_DK, :]
    return y, (conv_new, s_a, kb, vb, lf[:, :, LANE_F_B:LANE_F_B + B_HEADS], s_c), kv_caches


def _layer(x, xb, n, t, conv_buf8, s_a0, past_b, s_c0t, w, layer, depth, alpha, kv_caches=None):
    (v1, s1, s2), st, kv_caches = _mixer(x, xb, n, t, conv_buf8, s_a0, past_b, s_c0t, w, layer, depth,
                                         alpha, kv_caches)
    x1b, = _ln_apply(v1, s1, s2, w["ln1_g"], w["ln1_b"], False)
    hmid = _swiglu(x1b, w["w_gate"], w["w_up"], layer, 2048, 256)
    v2, t1, t2 = _down_proj(hmid, w["w_down"], layer, v1, s1, s2, w["ln1_g"], w["ln1_b"], alpha, 512, 512)
    xb, x = _ln_apply(v2, t1, t2, w["ln2_g"], w["ln2_b"], True)
    return x, xb, st, kv_caches


def kernel(x_prompt, x_sample, state_a_conv, state_a_rec, cache_b_k, cache_b_v, cache_b_logf, state_c_rec, w_in, conv_w, a_log, dt_bias, a_norm_g, f_bias, c_w2, c_b2, c_norm_g, w_out, ln1_g, ln1_b, w_gate, w_up, w_down, ln2_g, ln2_b):
    depth = w_in.shape[0]
    alpha = (2 * depth) ** DEPTH_ALPHA_POW
    nb, tp, _ = x_prompt.shape
    ns, ts, _ = x_sample.shape

    hp = x_prompt.reshape(nb * tp, D_MODEL).astype(F32)
    hs = x_sample.reshape(ns * ts, D_MODEL).astype(F32)
    hpb = hp.astype(BF16)
    hsb = hs.astype(BF16)
    p_states, s_states = [], []
    w_big, w_gate4 = _prep_w_in(w_in)
    w_out_b, w_gate_b, w_up_b, w_down_b = (a.astype(BF16) for a in (w_out, w_gate, w_up, w_down))
    past_len = cache_b_k.shape[2]
    cache_k = cache_b_k.reshape(depth, ns, past_len * B_HEADS, HEAD_DIM).astype(F32)
    cache_v = cache_b_v.reshape(depth, ns, past_len * B_HEADS, HEAD_DIM).astype(F32)
    prompt_kv = None
    for l in range(depth):
        w = {
            "w_big": w_big, "w_gate4": w_gate4,
            "conv_w": conv_w[l].astype(F32),
            "a_log": _pad_vec(a_log[l]), "dt_bias": _pad_vec(dt_bias[l]),
            "a_norm_g": a_norm_g[l].reshape(1, HEAD_DIM).astype(F32),
            "f_bias": jnp.pad(f_bias[l].reshape(1, B_HEADS).astype(F32),
                              ((0, 0), (LANE_F_B, LANES - LANE_F_B - B_HEADS))),
            "c_w2p": jnp.pad(_pad_heads_c(c_w2[l]),
                             ((LANE_LR_C, LANES - LANE_LR_C - C_RANK), (0, 0))).astype(BF16),
            "c_b2p": _pad_heads_c(c_b2[l].reshape(1, C_KEY)).astype(F32),
            "c_norm_g": c_norm_g[l].reshape(1, C_DV).astype(F32),
            "w_out": w_out_b,
            "ln1_g": ln1_g[l], "ln1_b": ln1_b[l], "ln2_g": ln2_g[l], "ln2_b": ln2_b[l],
            "w_gate": w_gate_b, "w_up": w_up_b, "w_down": w_down_b,
        }
        hp, hpb, stp, prompt_kv = _layer(
            hp, hpb, nb, tp,
            jnp.zeros((nb, SUBLANES, A_CONV_DIM), F32),
            jnp.zeros((nb, A_HEADS, HEAD_DIM, HEAD_DIM), F32),
            None,
            jnp.zeros((nb, C_HEADS, C_DV, C_DK_PAD), F32), w, l, depth, alpha, prompt_kv)
        p_states.append(stp)
        buf8 = jnp.pad(state_a_conv[l].astype(F32), ((0, 0), (SUBLANES - (CONV_W - 1), 0), (0, 0)))
        s_c0t = jnp.pad(jnp.swapaxes(state_c_rec[l].astype(F32), 2, 3),
                        ((0, 0), (0, 0), (0, 0), (0, C_DK_PAD - C_DK)))
        hs, hsb, sts, _ = _layer(
            hs, hsb, ns, ts, buf8, state_a_rec[l].astype(F32),
            (cache_k, cache_v, cache_b_logf[l]), s_c0t, w, l, depth, alpha)
        s_states.append(sts)

    dp, ds = x_prompt.dtype, x_sample.dtype
    stack = lambda states, i, dt: jnp.stack([s[i] for s in states], axis=0).astype(dt)
    prompt_k, prompt_v = (a.reshape(depth, nb, tp, B_HEADS, HEAD_DIM).astype(dp) for a in prompt_kv)
    return ((hp.reshape(nb, tp, D_MODEL).astype(dp), hs.reshape(ns, ts, D_MODEL).astype(ds))
            + (stack(p_states, 0, dp), stack(p_states, 1, dp), prompt_k, prompt_v,
               stack(p_states, 4, dp), stack(p_states, 5, dp))
            + tuple(stack(s_states, i, ds) for i in range(6)))
```

```python
import functools

import jax
import jax.numpy as jnp
from jax import lax
from jax.experimental import pallas as pl
from jax.experimental.pallas import tpu as pltpu

F32 = jnp.float32
BF16 = jnp.bfloat16

D_MODEL = 4096
GDN_CHUNK = 64
GLA_CHUNK = 128
HEAD_DIM = 128
A_HEADS = 12
A_KEY = A_HEADS * HEAD_DIM
A_VAL = A_HEADS * HEAD_DIM
A_CONV_DIM = 2 * A_KEY + A_VAL
CONV_W = 4
B_HEADS = 8
B_WIDTH = B_HEADS * HEAD_DIM
C_HEADS = 4
C_DV = 384
C_DK = 192
C_DK_PAD = 256
C_KEY = C_HEADS * C_DK
C_KEY_PAD = C_HEADS * C_DK_PAD
C_VAL = C_HEADS * C_DV
C_RANK = 16
C_TAU = 16.0
D_FF = 11008
DEPTH_ALPHA_POW = 0.25
LN_EPS = 1e-5
RMS_EPS = 1e-6

LANES = 128
SUBLANES = 8
VMEM_LIMIT = 56 * 1024 * 1024

COL_QKV_A = 0
COL_Z_A = 4608
COL_V_C = 6144
COL_R_C = 7680
COL_Q_B = 9216
COL_K_B = 10240
COL_V_B = 11264
COL_Q_C = 12288
COL_K_C = 13312
IN_BIG = 14336
IN_GATE = LANES
LANE_A_A = 0
LANE_B_A = LANE_A_A + A_HEADS
LANE_F_B = LANE_B_A + A_HEADS
LANE_LR_C = LANE_F_B + B_HEADS

NEG_BIG = -1e30
LOG2E = 1.4426950408889634
FOX_HEADS_PER_STEP = 4
FGATE_SUB = 256
GDN_SEQS_PER_STEP = 1
GDN_HEADS_PER_GROUP = 3


def _cparams(sem):
    return pltpu.CompilerParams(dimension_semantics=sem, vmem_limit_bytes=VMEM_LIMIT)


def _sigmoid(x):
    return 1.0 / (1.0 + jnp.exp(-x))


def _silu(x):
    return x * _sigmoid(x)


def _softplus(x):
    return jnp.maximum(x, 0.0) + jnp.log(1.0 + jnp.exp(-jnp.abs(x)))


def _log_sigmoid(x):
    return -_softplus(-x)


def _split3(x):
    x1 = x.astype(BF16)
    r1 = x - x1.astype(F32)
    x2 = r1.astype(BF16)
    x3 = (r1 - x2.astype(F32)).astype(BF16)
    return x1, x2, x3


def _dot(a, b):
    return jnp.dot(a, b, preferred_element_type=F32)


def _dot_nt(a, b):
    return lax.dot_general(a, b, (((1,), (1,)), ((), ())), preferred_element_type=F32)


def _dot_tn(a, b):
    return lax.dot_general(a, b, (((0,), (0,)), ((), ())), preferred_element_type=F32)


def _dot_exact_lhs(a_bf16, x):
    x1, x2, x3 = _split3(x)
    return _dot(a_bf16, x1) + _dot(a_bf16, x2) + _dot(a_bf16, x3)


def _solve_dot(a, b):
    return _dot(a.astype(BF16), b.astype(BF16))


def _iota2(shape, dim):
    return lax.broadcasted_iota(jnp.int32, shape, dim)


def _tri_incl(n):
    return (_iota2((n, n), 1) <= _iota2((n, n), 0)).astype(BF16)


def _mm_kernel(x_ref, w_ref, o_ref):
    o_ref[...] = _dot(x_ref[...], w_ref[...]).astype(o_ref.dtype)


def _matmul(x, w, layer, tm, tn, out_dtype):
    m, k = x.shape
    n = w.shape[2]
    tm = min(tm, m)
    return pl.pallas_call(
        _mm_kernel,
        grid=(m // tm, n // tn),
        in_specs=[pl.BlockSpec((tm, k), lambda i, j: (i, 0)),
                  pl.BlockSpec((None, k, tn), lambda i, j: (layer, 0, j))],
        out_specs=pl.BlockSpec((tm, tn), lambda i, j: (i, j)),
        out_shape=jax.ShapeDtypeStruct((m, n), out_dtype),
        compiler_params=_cparams(("parallel", "parallel")),
    )(x, w)


def _lane_partial(v):
    part = v[:, 0:LANES]
    for c in range(1, v.shape[1] // LANES):
        part = part + v[:, c * LANES:(c + 1) * LANES]
    return part


def _residual_epilogue(alpha, r, acc, v_ref, s1_ref, s2_ref):
    v = alpha * r + acc
    v_ref[...] = v
    p1 = _lane_partial(v)
    p2 = _lane_partial(v * v)
    first = pl.program_id(1) == 0
    s1_ref[...] = jnp.where(first, p1, s1_ref[...] + p1)
    s2_ref[...] = jnp.where(first, p2, s2_ref[...] + p2)


def _row_stats(s1, s2, width):
    mu = jnp.sum(s1, axis=-1, keepdims=True) * (1.0 / width)
    var = jnp.sum(s2, axis=-1, keepdims=True) * (1.0 / width) - mu * mu
    return mu, lax.rsqrt(var + LN_EPS)


def _outproj_kernel(alpha, a_ref, b_ref, c_ref, w_ref, r_ref, v_ref, s1_ref, s2_ref):
    acc = (_dot(a_ref[...], w_ref[0:A_VAL, :])
           + _dot(b_ref[...], w_ref[A_VAL:A_VAL + B_WIDTH, :])
           + _dot(c_ref[...], w_ref[A_VAL + B_WIDTH:D_MODEL, :]))
    _residual_epilogue(alpha, r_ref[...], acc, v_ref, s1_ref, s2_ref)


def _resid_out(m, tm, tn):
    tile = pl.BlockSpec((tm, tn), lambda i, j: (i, j))
    stat = pl.BlockSpec((tm, LANES), lambda i, j: (i, 0))
    return ([tile, stat, stat],
            [jax.ShapeDtypeStruct((m, D_MODEL), F32)] + [jax.ShapeDtypeStruct((m, LANES), F32)] * 2)


def _outproj(oa, ob, oc, w, layer, resid, alpha, tm, tn):
    m = oa.shape[0]
    tm = min(tm, m)
    rows = lambda width: pl.BlockSpec((tm, width), lambda i, j: (i, 0))
    out_specs, out_shape = _resid_out(m, tm, tn)
    return pl.pallas_call(
        functools.partial(_outproj_kernel, alpha),
        grid=(m // tm, D_MODEL // tn),
        in_specs=[rows(A_VAL), rows(B_WIDTH), rows(C_VAL),
                  pl.BlockSpec((None, D_MODEL, tn), lambda i, j: (layer, 0, j)),
                  pl.BlockSpec((tm, tn), lambda i, j: (i, j))],
        out_specs=out_specs, out_shape=out_shape,
        compiler_params=_cparams(("parallel", "arbitrary")),
    )(oa, ob, oc, w, resid)


def _down_kernel(alpha, h_ref, w_ref, v1_ref, s1_ref, s2_ref, g_ref, b_ref, v_ref, t1_ref, t2_ref):
    acc = _dot(h_ref[...], w_ref[...])
    mu, rstd = _row_stats(s1_ref[...], s2_ref[...], D_MODEL)
    x1 = (v1_ref[...] - mu) * rstd * g_ref[...] + b_ref[...]
    _residual_epilogue(alpha, x1, acc, v_ref, t1_ref, t2_ref)


def _down_proj(hmid, w, layer, v1, s1, s2, g, b, alpha, tm, tn):
    m, k = hmid.shape
    tm = min(tm, m)
    tile = pl.BlockSpec((tm, tn), lambda i, j: (i, j))
    stat = pl.BlockSpec((tm, LANES), lambda i, j: (i, 0))
    vec = pl.BlockSpec((1, tn), lambda i, j: (0, j))
    out_specs, out_shape = _resid_out(m, tm, tn)
    return pl.pallas_call(
        functools.partial(_down_kernel, alpha),
        grid=(m // tm, D_MODEL // tn),
        in_specs=[pl.BlockSpec((tm, k), lambda i, j: (i, 0)),
                  pl.BlockSpec((None, k, tn), lambda i, j: (layer, 0, j)),
                  tile, stat, stat, vec, vec],
        out_specs=out_specs, out_shape=out_shape,
        compiler_params=_cparams(("parallel", "arbitrary")),
    )(hmid, w, v1, s1, s2, g.reshape(1, D_MODEL), b.reshape(1, D_MODEL))


def _swiglu_kernel(x_ref, wg_ref, wu_ref, o_ref):
    x = x_ref[...]
    a = _dot(x, wg_ref[...].astype(BF16))
    b = _dot(x, wu_ref[...].astype(BF16))
    o_ref[...] = (_silu(a) * b).astype(o_ref.dtype)


def _swiglu(x, wg, wu, layer, tm, tn):
    m, k = x.shape
    n = wg.shape[2]
    tm = min(tm, m)
    wspec = pl.BlockSpec((None, k, tn), lambda i, j: (layer, 0, j))
    return pl.pallas_call(
        _swiglu_kernel,
        grid=(m // tm, n // tn),
        in_specs=[pl.BlockSpec((tm, k), lambda i, j: (i, 0), pipeline_mode=pl.Buffered(1)), wspec, wspec],
        out_specs=pl.BlockSpec((tm, tn), lambda i, j: (i, j)),
        out_shape=jax.ShapeDtypeStruct((m, n), BF16),
        compiler_params=_cparams(("parallel", "parallel")),
    )(x, wg, wu)


def _ln_apply_kernel(with_f32, v_ref, s1_ref, s2_ref, g_ref, b_ref, *out_refs):
    mu, rstd = _row_stats(s1_ref[...], s2_ref[...], D_MODEL)
    out = (v_ref[...] - mu) * rstd * g_ref[...] + b_ref[...]
    out_refs[0][...] = out.astype(BF16)
    if with_f32:
        out_refs[1][...] = out


def _ln_apply(v, s1, s2, g, b, with_f32, tm=256):
    m, d = v.shape
    tm = min(tm, m)
    row = pl.BlockSpec((tm, d), lambda i: (i, 0))
    stat = pl.BlockSpec((tm, LANES), lambda i: (i, 0))
    vec = pl.BlockSpec((1, d), lambda i: (0, 0))
    n_out = 2 if with_f32 else 1
    return pl.pallas_call(
        functools.partial(_ln_apply_kernel, with_f32),
        grid=(m // tm,),
        in_specs=[row, stat, stat, vec, vec],
        out_specs=[row] * n_out,
        out_shape=[jax.ShapeDtypeStruct((m, d), BF16), jax.ShapeDtypeStruct((m, d), F32)][:n_out],
        compiler_params=_cparams(("parallel",)),
    )(v, s1, s2, g.reshape(1, d), b.reshape(1, d))


def _fgate_kernel(apply_gate, tb, x_ref, bias_ref, c0_ref, lf_ref, c_ref, carry):
    @pl.when(pl.program_id(1) == 0)
    def _():
        carry[...] = c0_ref[0]

    x = x_ref[0]
    lf = _log_sigmoid(x + bias_ref[...]) if apply_gate else x
    lf_ref[0] = lf
    sub = min(tb, FGATE_SUB)
    tri = _tri_incl(sub)
    parts = [_dot_exact_lhs(tri, lf[k * sub:(k + 1) * sub]) for k in range(tb // sub)]
    run = carry[...]
    for k, part in enumerate(parts):
        c_ref[0, k * sub:(k + 1) * sub, :] = part + run
        run = run + part[sub - 1:sub, :]
    carry[...] = run


def _fgate(x, bias, c0, apply_gate, tb):
    n, t, _ = x.shape
    tb = min(tb, t)
    blk = pl.BlockSpec((1, tb, LANES), lambda i, j: (i, j, 0))
    return pl.pallas_call(
        functools.partial(_fgate_kernel, apply_gate, tb),
        grid=(n, t // tb),
        in_specs=[pl.BlockSpec((1, tb, LANES), lambda i, j: (i, j, 0)),
                  pl.BlockSpec((1, LANES), lambda i, j: (0, 0)),
                  pl.BlockSpec((1, 1, LANES), lambda i, j: (i, 0, 0))],
        out_specs=[blk, blk],
        out_shape=[jax.ShapeDtypeStruct((n, t, LANES), F32)] * 2,
        scratch_shapes=[pltpu.VMEM((1, LANES), F32)],
        compiler_params=_cparams(("parallel", "arbitrary")),
    )(x, bias, c0)


def _gdn_kernel(L, NB, qkv_ref, z_ref, gate_ref, buf_ref, convw_ref, alog_ref, dtb_ref, ng_ref, s0_ref,
                o_ref, s_ref, xs):
    @pl.when(pl.program_id(1) == 0)
    def _():
        xs[:, 0:SUBLANES, :] = buf_ref[...]
        s_ref[...] = s0_ref[...]

    xs[:, SUBLANES:SUBLANES + L, :] = qkv_ref[...]

    def conv_cols(b, c0):
        acc = xs[b, 5:5 + L, c0:c0 + HEAD_DIM] * convw_ref[0:1, c0:c0 + HEAD_DIM]
        for i in range(1, CONV_W):
            acc = acc + xs[b, 5 + i:5 + i + L, c0:c0 + HEAD_DIM] * convw_ref[i:i + 1, c0:c0 + HEAD_DIM]
        return _silu(acc)

    def l2n(x):
        return x * lax.rsqrt(jnp.sum(x * x, axis=-1, keepdims=True) + RMS_EPS)

    tri = _tri_incl(L)
    eye_l = (_iota2((LANES, LANES), 0) == _iota2((LANES, LANES), 1)).astype(BF16)
    seqs = []
    for b in range(NB):
        a_in = gate_ref[b]
        g = -jnp.exp(alog_ref[...]) * _softplus(a_in + dtb_ref[...])
        gc = _dot_exact_lhs(tri, g)
        g1, g2, g3 = _split3(gc)
        gct = _dot_nt(eye_l, g1) + _dot_nt(eye_l, g2) + _dot_nt(eye_l, g3)
        seqs.append(dict(gc=gc, gct=gct, beta=_sigmoid(a_in)))

    row = _iota2((L, L), 0)
    col = _iota2((L, L), 1)
    eye = (row == col).astype(F32)
    n_lvl = L.bit_length() - 2

    mm = _solve_dot

    def st_prep(hd):
        b, h = hd["b"], hd["h"]
        sq = seqs[b]
        q = l2n(conv_cols(b, h * HEAD_DIM)) * (HEAD_DIM ** -0.5)
        k = l2n(conv_cols(b, A_KEY + h * HEAD_DIM))
        v = conv_cols(b, 2 * A_KEY + h * HEAD_DIM)
        gcol = sq["gc"][:, h:h + 1]
        bcol = sq["beta"][:, LANE_B_A + h:LANE_B_A + h + 1]
        glast = sq["gc"][L - 1:L, h:h + 1]
        dec = jnp.exp(jnp.where(row >= col, gcol - sq["gct"][h:h + 1, :], NEG_BIG))
        eg = jnp.exp(gcol)
        qb = q.astype(BF16)
        kb = k.astype(BF16)
        p = -(bcol * _dot_nt(kb, kb) * jnp.where(row > col, dec, 0.0))
        hd.update(qb=qb, p=p, t=eye + p, eg=eg, e_last=jnp.exp(glast),
                  qkd=(_dot_nt(qb, kb) * dec).astype(BF16),
                  rhs=jnp.concatenate([(bcol * eg) * k, bcol * v], axis=1),
                  k_end=(k * jnp.exp(glast - gcol)).astype(BF16),
                  s=s_ref[b, h])

    def st_square(hd):
        hd["p"] = mm(hd["p"], hd["p"])

    def st_level(hd):
        pt = mm(hd["p"], jnp.concatenate([hd["p"], hd["t"]], axis=1))
        hd["t"] = hd["t"] + pt[:, L:2 * L]
        hd["p"] = pt[:, 0:L]

    def st_last_level(hd):
        hd["t"] = hd["t"] + mm(hd["p"], hd["t"])

    def st_solve(hd):
        hd["tr"] = mm(hd["t"], hd["rhs"])

    def st_state(hd):
        tr = hd["tr"]
        sb = hd["s"].astype(BF16)
        hd["ub"] = (tr[:, HEAD_DIM:2 * HEAD_DIM] - _dot(tr[:, 0:HEAD_DIM].astype(BF16), sb)).astype(BF16)
        hd["qs"] = _dot(hd["qb"], sb)

    def st_out(hd):
        b, h = hd["b"], hd["h"]
        o = hd["eg"] * hd["qs"] + _dot(hd["qkd"], hd["ub"])
        s_ref[b, h] = hd["e_last"] * hd["s"] + _dot_tn(hd["k_end"], hd["ub"])
        o = o * lax.rsqrt(jnp.mean(o * o, axis=-1, keepdims=True) + RMS_EPS) * ng_ref[...]
        zh = z_ref[b, :, h * HEAD_DIM:(h + 1) * HEAD_DIM]
        o_ref[b, :, h * HEAD_DIM:(h + 1) * HEAD_DIM] = (o * _silu(zh)).astype(BF16)
        hd.clear()

    stages = [st_prep, st_square] + [st_level] * (n_lvl - 1) + [st_last_level, st_solve, st_state, st_out]
    heads = [dict(b=b, h=h) for b in range(NB) for h in range(A_HEADS)]
    groups = [heads[i:i + GDN_HEADS_PER_GROUP] for i in range(0, len(heads), GDN_HEADS_PER_GROUP)]
    for wave in range(len(groups) + len(stages) - 1):
        for g, group in enumerate(groups):
            if 0 <= wave - g < len(stages):
                for hd in group:
                    stages[wave - g](hd)

    tail = xs[:, L:L + SUBLANES, :]
    xs[:, 0:SUBLANES, :] = tail


def _gdn(hbig, hgate, conv_buf8, conv_w, a_log, dt_bias, norm_g, s0):
    n, t, _ = hbig.shape
    L = min(t, GDN_CHUNK)
    nb = GDN_SEQS_PER_STEP
    const2 = lambda i, j: (0, 0)
    state = pl.BlockSpec((nb, A_HEADS, HEAD_DIM, HEAD_DIM), lambda i, j: (i, 0, 0, 0))
    return pl.pallas_call(
        functools.partial(_gdn_kernel, L, nb),
        grid=(n // nb, t // L),
        in_specs=[pl.BlockSpec((nb, L, A_CONV_DIM), lambda i, j: (i, j, COL_QKV_A // A_CONV_DIM)),
                  pl.BlockSpec((nb, L, A_VAL), lambda i, j: (i, j, COL_Z_A // A_VAL)),
                  pl.BlockSpec((nb, L, LANES), lambda i, j: (i, j, 0)),
                  pl.BlockSpec((nb, SUBLANES, A_CONV_DIM), lambda i, j: (i, 0, 0)),
                  pl.BlockSpec((CONV_W, A_CONV_DIM), const2),
                  pl.BlockSpec((1, LANES), const2),
                  pl.BlockSpec((1, LANES), const2),
                  pl.BlockSpec((1, HEAD_DIM), const2),
                  state],
        out_specs=[pl.BlockSpec((nb, L, A_VAL), lambda i, j: (i, j, 0)), state],
        out_shape=[jax.ShapeDtypeStruct((n, t, A_VAL), BF16),
                   jax.ShapeDtypeStruct((n, A_HEADS, HEAD_DIM, HEAD_DIM), F32)],
        scratch_shapes=[pltpu.VMEM((nb, L + SUBLANES, A_CONV_DIM), F32)],
        compiler_params=_cparams(("parallel", "arbitrary")),
    )(hbig, hbig, hgate, conv_buf8, conv_w, a_log, dt_bias, norm_g, s0)


def _gla_kernel(L, q_ref, k_ref, v_ref, r_ref, gate_ref, w2_ref, b2_ref, ng_ref, s0_ref,
                o_ref, s_ref):
    @pl.when(pl.program_id(1) == 0)
    def _():
        s_ref[0] = s0_ref[0]

    z = _dot(gate_ref[0].astype(BF16), w2_ref[...]) + b2_ref[...]
    lg = _log_sigmoid(z) * (1.0 / C_TAU)
    b = _dot_exact_lhs(_tri_incl(L), lg)
    causal = _iota2((L, L), 0) >= _iota2((L, L), 1)

    heads = []
    for h in range(C_HEADS):
        ks = slice(h * C_DK_PAD, (h + 1) * C_DK_PAD)
        vs = slice(h * C_DV, (h + 1) * C_DV)
        bh = b[:, ks]
        blast = bh[L - 1:L, :]
        bref = bh[L // 2:L // 2 + 1, :]
        q = q_ref[0, :, ks] * (C_DK ** -0.5)
        k = k_ref[0, :, ks]
        heads.append(dict(
            vs=vs, vb=v_ref[0, :, vs].astype(BF16), st=s_ref[0, h],
            q_state=(q * jnp.exp(bh)).astype(BF16),
            q_in=(q * jnp.exp(bh - bref)).astype(BF16),
            k_in=(k * jnp.exp(bref - bh)).astype(BF16),
            k_end=(k * jnp.exp(blast - bh)).astype(BF16),
            e_last=jnp.exp(blast)))
    for hd in heads:
        hd["att"] = jnp.where(causal, _dot_nt(hd["q_in"], hd["k_in"]), 0.0).astype(BF16)
        hd["o"] = _dot_nt(hd["q_state"], hd["st"].astype(BF16))
        hd["st_new"] = hd["st"] * hd["e_last"] + _dot_tn(hd["vb"], hd["k_end"])
    for hd in heads:
        o = hd["o"] + _dot(hd["att"], hd["vb"])
        hd["o"] = o * lax.rsqrt(jnp.mean(o * o, axis=-1, keepdims=True) + RMS_EPS) * ng_ref[...]
    for h, hd in enumerate(heads):
        s_ref[0, h] = hd["st_new"]
        o_ref[0, :, hd["vs"]] = (hd["o"] * _silu(r_ref[0, :, hd["vs"]])).astype(BF16)


def _gla(hbig, hgate, w2p, b2p, norm_g, s0t):
    n, t, _ = hbig.shape
    L = min(t, GLA_CHUNK)
    const2 = lambda i, j: (0, 0)
    state = pl.BlockSpec((1, C_HEADS, C_DV, C_DK_PAD), lambda i, j: (i, 0, 0, 0))
    return pl.pallas_call(
        functools.partial(_gla_kernel, L),
        grid=(n, t // L),
        in_specs=[pl.BlockSpec((1, L, C_KEY_PAD), lambda i, j: (i, j, COL_Q_C // C_KEY_PAD)),
                  pl.BlockSpec((1, L, C_KEY_PAD), lambda i, j: (i, j, COL_K_C // C_KEY_PAD)),
                  pl.BlockSpec((1, L, C_VAL), lambda i, j: (i, j, COL_V_C // C_VAL)),
                  pl.BlockSpec((1, L, C_VAL), lambda i, j: (i, j, COL_R_C // C_VAL)),
                  pl.BlockSpec((1, L, LANES), lambda i, j: (i, j, 0)),
                  pl.BlockSpec((LANES, C_KEY_PAD), const2),
                  pl.BlockSpec((1, C_KEY_PAD), const2),
                  pl.BlockSpec((1, C_DV), const2),
                  state],
        out_specs=[pl.BlockSpec((1, L, C_VAL), lambda i, j: (i, j, 0)), state],
        out_shape=[jax.ShapeDtypeStruct((n, t, C_VAL), BF16),
                   jax.ShapeDtypeStruct((n, C_HEADS, C_DV, C_DK_PAD), F32)],
        compiler_params=_cparams(("parallel", "arbitrary")),
    )(hbig, hbig, hbig, hbig, hgate, w2p, b2p, norm_g, s0t)


def _fox_prep_kernel(q_ref, k_ref, v_ref, c_ref, qp_ref, kp_ref, vp_ref):
    tb = q_ref.shape[1]
    lane = _iota2((tb, LANES), 1)
    ones_q = jnp.where((lane >= 3) & (lane < 6), 1.0, 0.0)
    ones_k = jnp.where(lane < 3, 1.0, 0.0)
    c2 = c_ref[0] * LOG2E
    for h in range(B_HEADS):
        hs = slice(h * HEAD_DIM, (h + 1) * HEAD_DIM)
        c1, cm, cl = (t.astype(F32) for t in _split3(c2[:, LANE_F_B + h:LANE_F_B + h + 1]))
        ext_q = jnp.where(lane == 0, c1, jnp.where(lane == 1, cm, jnp.where(lane == 2, cl, ones_q)))
        ext_k = jnp.where(lane == 3, -c1, jnp.where(lane == 4, -cm, jnp.where(lane == 5, -cl, ones_k)))
        base = 2 * h * HEAD_DIM
        qp_ref[0, :, base:base + HEAD_DIM] = (q_ref[0, :, hs] * (HEAD_DIM ** -0.5 * LOG2E)).astype(BF16)
        qp_ref[0, :, base + HEAD_DIM:base + 2 * HEAD_DIM] = ext_q.astype(BF16)
        kp_ref[0, :, base:base + HEAD_DIM] = k_ref[0, :, hs].astype(BF16)
        kp_ref[0, :, base + HEAD_DIM:base + 2 * HEAD_DIM] = ext_k.astype(BF16)
    vp_ref[0] = v_ref[0].astype(BF16)


def _fox_prep_cache_kernel(q_ref, k_ref, v_ref, c_ref, *rest):
    qp_ref, kp_ref, vp_ref, kc_ref, vc_ref = rest[-5:]
    _fox_prep_kernel(q_ref, k_ref, v_ref, c_ref, qp_ref, kp_ref, vp_ref)
    kc_ref[0] = k_ref[0]
    vc_ref[0] = v_ref[0]


def _fox_prep(hbig, c, layer, depth, caches, tb=256):
    n, t, _ = hbig.shape
    tb = min(tb, t)
    col = lambda blk: (lambda i, j: (i, j, blk))
    wide = pl.BlockSpec((1, tb, 2 * B_WIDTH), col(0))
    cache_spec = pl.BlockSpec((None, 1, tb, B_WIDTH), lambda i, j: (layer, i, j, 0))
    cache_shape = jax.ShapeDtypeStruct((depth, n, t, B_WIDTH), F32)
    in_specs = [pl.BlockSpec((1, tb, B_WIDTH), col(COL_Q_B // B_WIDTH)),
                pl.BlockSpec((1, tb, B_WIDTH), col(COL_K_B // B_WIDTH)),
                pl.BlockSpec((1, tb, B_WIDTH), col(COL_V_B // B_WIDTH)),
                pl.BlockSpec((1, tb, LANES), col(0))]
    args = [hbig, hbig, hbig, c]
    aliases = {}
    if caches is not None:
        in_specs += [pl.BlockSpec(memory_space=pl.ANY)] * 2
        args += list(caches)
        aliases = {4: 3, 5: 4}
    return pl.pallas_call(
        _fox_prep_cache_kernel,
        grid=(n, t // tb),
        in_specs=in_specs,
        out_specs=[wide, wide, pl.BlockSpec((1, tb, B_WIDTH), col(0)), cache_spec, cache_spec],
        out_shape=[jax.ShapeDtypeStruct((n, t, 2 * B_WIDTH), BF16),
                   jax.ShapeDtypeStruct((n, t, 2 * B_WIDTH), BF16),
                   jax.ShapeDtypeStruct((n, t, B_WIDTH), BF16),
                   cache_shape, cache_shape],
        input_output_aliases=aliases,
        compiler_params=_cparams(("parallel", "parallel")),
    )(*args)


def _fox_prompt_kernel(tq, q_ref, k_ref, v_ref, o_ref, m_sc, l_sc, acc_sc):
    qi = pl.program_id(2)
    m_sc[...] = jnp.full(m_sc.shape, NEG_BIG, F32)
    l_sc[...] = jnp.zeros(l_sc.shape, F32)
    acc_sc[...] = jnp.zeros(acc_sc.shape, F32)
    n_tiles = tq // LANES
    heads = range(FOX_HEADS_PER_STEP)
    qw = 2 * HEAD_DIM

    def block(j, masked):
        off = pl.multiple_of(j * tq, tq)
        ss = [_dot_nt(q_ref[0, :, h * qw:(h + 1) * qw], k_ref[0, pl.ds(off, tq), h * qw:(h + 1) * qw])
              for h in heads]
        if masked:
            visible = _iota2((tq, tq), 1) <= _iota2((tq, tq), 0)
            ss = [jnp.where(visible, s, NEG_BIG) for s in ss]
        new = []
        for h, s in zip(heads, ss):
            tiles = [s[:, c * LANES:(c + 1) * LANES] for c in range(n_tiles)]
            mx = tiles[0]
            for tl in tiles[1:]:
                mx = jnp.maximum(mx, tl)
            m_prev = m_sc[h]
            m_new = jnp.maximum(m_prev, jnp.max(mx, axis=-1, keepdims=True))
            alpha = jnp.exp2(m_prev - m_new)
            ps = [jnp.exp2(tl - m_new) for tl in tiles]
            psum = ps[0]
            for pt in ps[1:]:
                psum = psum + pt
            p = jnp.concatenate([pt.astype(BF16) for pt in ps], axis=1)
            pv = _dot(p, v_ref[0, pl.ds(off, tq), h * HEAD_DIM:(h + 1) * HEAD_DIM])
            new.append((m_new, alpha * l_sc[h] + psum, alpha * acc_sc[h] + pv))
        for h, (m_new, l_new, acc_new) in zip(heads, new):
            m_sc[h] = m_new
            l_sc[h] = l_new
            acc_sc[h] = acc_new

    def body(j, carry):
        block(j, False)
        return carry

    lax.fori_loop(0, qi, body, 0)
    block(qi, True)
    for h in heads:
        o_ref[0, :, h * HEAD_DIM:(h + 1) * HEAD_DIM] = (
            acc_sc[h] / jnp.sum(l_sc[h], axis=-1, keepdims=True)).astype(BF16)


def _fox_prompt(qp, kp, vp, tq=512):
    n, t, _ = vp.shape
    tq = min(tq, t)
    hps = FOX_HEADS_PER_STEP
    return pl.pallas_call(
        functools.partial(_fox_prompt_kernel, tq),
        grid=(n, B_HEADS // hps, t // tq),
        in_specs=[pl.BlockSpec((1, tq, hps * 2 * HEAD_DIM), lambda i, h, qi: (i, qi, h)),
                  pl.BlockSpec((1, t, hps * 2 * HEAD_DIM), lambda i, h, qi: (i, 0, h),
                               pipeline_mode=pl.Buffered(1)),
                  pl.BlockSpec((1, t, hps * HEAD_DIM), lambda i, h, qi: (i, 0, h),
                               pipeline_mode=pl.Buffered(1))],
        out_specs=pl.BlockSpec((1, tq, hps * HEAD_DIM), lambda i, h, qi: (i, qi, h)),
        out_shape=jax.ShapeDtypeStruct((n, t, B_WIDTH), BF16),
        scratch_shapes=[pltpu.VMEM((hps, tq, LANES), F32),
                        pltpu.VMEM((hps, tq, LANES), F32),
                        pltpu.VMEM((hps, tq, HEAD_DIM), F32)],
        compiler_params=_cparams(("parallel", "parallel", "arbitrary")),
    )(qp, kp, vp)


def _fox_sample_kernel(t, p, q_ref, kn_ref, vn_ref, kp_ref, vp_ref, cq_ref, ckp_ref, ckn_ref, o_ref):
    causal = _iota2((t, t), 1) <= _iota2((t, t), 0)
    for h in range(B_HEADS):
        hs = slice(h * HEAD_DIM, (h + 1) * HEAD_DIM)
        kp = kp_ref[0, pl.ds(h, p, stride=B_HEADS), :].astype(BF16)
        vp = vp_ref[0, pl.ds(h, p, stride=B_HEADS), :].astype(BF16)
        qb = (q_ref[0, :, hs] * (HEAD_DIM ** -0.5)).astype(BF16)
        cq = cq_ref[0, h]
        sp = _dot_nt(qb, kp) + (cq - ckp_ref[0, h])
        sn = _dot_nt(qb, kn_ref[0, :, hs].astype(BF16)) + (cq - ckn_ref[0, h])
        sn = jnp.where(causal, sn, NEG_BIG)
        m = jnp.maximum(jnp.max(sp, axis=-1, keepdims=True), jnp.max(sn, axis=-1, keepdims=True))
        pp = jnp.exp(sp - m)
        pn = jnp.exp(sn - m)
        den = jnp.sum(pp, axis=-1, keepdims=True) + jnp.sum(pn, axis=-1, keepdims=True)
        o = _dot(pp.astype(BF16), vp) + _dot(pn.astype(BF16), vn_ref[0, :, hs].astype(BF16))
        o_ref[0, :, hs] = (o / den).astype(BF16)


def _fox_sample(hbig, k_past, v_past, layer, cq, ckp, ckn):
    n, t, _ = hbig.shape
    rows = k_past.shape[2]
    p = rows // B_HEADS
    col = lambda base: (lambda i: (i, 0, base // B_WIDTH))
    past = pl.BlockSpec((None, 1, rows, HEAD_DIM), lambda i: (layer, i, 0, 0))
    whole = lambda a: pl.BlockSpec((1,) + a.shape[1:], lambda i: (i, 0, 0, 0))
    return pl.pallas_call(
        functools.partial(_fox_sample_kernel, t, p),
        grid=(n,),
        in_specs=[pl.BlockSpec((1, t, B_WIDTH), col(COL_Q_B)),
                  pl.BlockSpec((1, t, B_WIDTH), col(COL_K_B)),
                  pl.BlockSpec((1, t, B_WIDTH), col(COL_V_B)),
                  past, past, whole(cq), whole(ckp), whole(ckn)],
        out_specs=pl.BlockSpec((1, t, B_WIDTH), lambda i: (i, 0, 0)),
        out_shape=jax.ShapeDtypeStruct((n, t, B_WIDTH), BF16),
        compiler_params=_cparams(("parallel",)),
    )(hbig, hbig, hbig, k_past, v_past, cq, ckp, ckn)


def _pad_cols(w, width):
    return jnp.pad(w, [(0, 0)] * (w.ndim - 1) + [(0, width - w.shape[-1])])


def _pad_heads_c(w):
    lead = w.shape[:-1]
    w = w.reshape(lead + (C_HEADS, C_DK))
    w = jnp.pad(w, [(0, 0)] * len(lead) + [(0, 0), (0, C_DK_PAD - C_DK)])
    return w.reshape(lead + (C_KEY_PAD,))


def _prep_w_in(w):
    o = 0
    parts = {}
    for name, size in (("qkv_a", A_CONV_DIM), ("z_a", A_VAL), ("a_a", A_HEADS), ("b_a", A_HEADS),
                       ("q_b", B_WIDTH), ("k_b", B_WIDTH), ("v_b", B_WIDTH), ("f_b", B_HEADS),
                       ("q_c", C_KEY), ("k_c", C_KEY), ("v_c", C_VAL), ("r_c", C_VAL), ("lr_c", C_RANK)):
        parts[name] = w[..., o:o + size].astype(BF16)
        o += size
    big = jnp.concatenate([parts["qkv_a"], parts["z_a"], parts["v_c"], parts["r_c"],
                           parts["q_b"], parts["k_b"], parts["v_b"],
                           _pad_heads_c(parts["q_c"]), _pad_heads_c(parts["k_c"])], axis=-1)
    gate = _pad_cols(jnp.concatenate([parts[nm] for nm in ("a_a", "b_a", "f_b", "lr_c")], axis=-1), LANES)
    return big, gate


def _pad_vec(v):
    return _pad_cols(v.reshape(1, -1).astype(F32), LANES)


def _mixer(x, xb, n, t, conv_buf8, s_a0, past_b, s_c0t, w, layer, depth, alpha, kv_caches):
    m = n * t
    hbig = _matmul(xb, w["w_big"], layer, 1024, 1024, F32).reshape(n, t, IN_BIG)
    hgate = _matmul(xb, w["w_gate4"], layer, 1024, IN_GATE, F32).reshape(n, t, IN_GATE)

    oa, s_a = _gdn(hbig, hgate, conv_buf8, w["conv_w"], w["a_log"], w["dt_bias"], w["a_norm_g"], s_a0)
    oc, s_ct = _gla(hbig, hgate, w["c_w2p"], w["c_b2p"], w["c_norm_g"], s_c0t)

    zero_c = jnp.zeros((n, 1, LANES), F32)
    if past_b is None:
        lf, c = _fgate(hgate, w["f_bias"], zero_c, True, 2048)
        qp, kp, vp, k_cache, v_cache = _fox_prep(hbig, c, layer, depth, kv_caches)
        kv_caches = (k_cache, v_cache)
        ob = _fox_prompt(qp, kp, vp)
        kb = vb = None
    else:
        k_past, v_past, lf_past = past_b
        p = lf_past.shape[1]
        lf_past = jnp.pad(lf_past.astype(F32), ((0, 0), (0, 0), (LANE_F_B, LANES - LANE_F_B - B_HEADS)))
        _, c_past = _fgate(lf_past, w["f_bias"], zero_c, False, 2048)
        lf, c = _fgate(hgate, w["f_bias"], c_past[:, p - 1:p, :], True, t)
        fb = slice(LANE_F_B, LANE_F_B + B_HEADS)
        cq = jnp.swapaxes(c[:, :, fb], 1, 2)[..., None]
        ckn = jnp.swapaxes(c[:, :, fb], 1, 2)[:, :, None, :]
        ckp = jnp.swapaxes(c_past[:, :, fb], 1, 2)[:, :, None, :]
        ob = _fox_sample(hbig, k_past, v_past, layer, cq, ckp, ckn)

    y = _outproj(oa.reshape(m, A_VAL), ob.reshape(m, B_WIDTH), oc.reshape(m, C_VAL),
                 w["w_out"], layer, x, alpha, 1024, 1024)

    conv_new = hbig[:, t - (CONV_W - 1):, COL_QKV_A:COL_QKV_A + A_CONV_DIM]
    if past_b is not None:
        kb = hbig[:, :, COL_K_B:COL_K_B + B_WIDTH].reshape(n, t, B_HEADS, HEAD_DIM)
        vb = hbig[:, :, COL_V_B:COL_V_B + B_WIDTH].reshape(n, t, B_HEADS, HEAD_DIM)
    s_c = jnp.swapaxes(s_ct, 2, 3)[:, :, :C_DK, :]
    return y, (conv_new, s_a, kb, vb, lf[:, :, LANE_F_B:LANE_F_B + B_HEADS], s_c), kv_caches


def _layer(x, xb, n, t, conv_buf8, s_a0, past_b, s_c0t, w, layer, depth, alpha, kv_caches=None):
    (v1, s1, s2), st, kv_caches = _mixer(x, xb, n, t, conv_buf8, s_a0, past_b, s_c0t, w, layer, depth,
                                         alpha, kv_caches)
    x1b, = _ln_apply(v1, s1, s2, w["ln1_g"], w["ln1_b"], False)
    hmid = _swiglu(x1b, w["w_gate"], w["w_up"], layer, 2048, 256)
    v2, t1, t2 = _down_proj(hmid, w["w_down"], layer, v1, s1, s2, w["ln1_g"], w["ln1_b"], alpha, 512, 512)
    xb, x = _ln_apply(v2, t1, t2, w["ln2_g"], w["ln2_b"], True)
    return x, xb, st, kv_caches


def kernel(x_prompt, x_sample, state_a_conv, state_a_rec, cache_b_k, cache_b_v, cache_b_logf, state_c_rec, w_in, conv_w, a_log, dt_bias, a_norm_g, f_bias, c_w2, c_b2, c_norm_g, w_out, ln1_g, ln1_b, w_gate, w_up, w_down, ln2_g, ln2_b):
    depth = w_in.shape[0]
    alpha = (2 * depth) ** DEPTH_ALPHA_POW
    nb, tp, _ = x_prompt.shape
    ns, ts, _ = x_sample.shape

    hp = x_prompt.reshape(nb * tp, D_MODEL).astype(F32)
    hs = x_sample.reshape(ns * ts, D_MODEL).astype(F32)
    hpb = hp.astype(BF16)
    hsb = hs.astype(BF16)
    p_states, s_states = [], []
    w_big, w_gate4 = _prep_w_in(w_in)
    w_out_b, w_down_b = w_out.astype(BF16), w_down.astype(BF16)
    past_len = cache_b_k.shape[2]
    cache_k = cache_b_k.reshape(depth, ns, past_len * B_HEADS, HEAD_DIM).astype(F32)
    cache_v = cache_b_v.reshape(depth, ns, past_len * B_HEADS, HEAD_DIM).astype(F32)
    prompt_kv = None
    for l in range(depth):
        w = {
            "w_big": w_big, "w_gate4": w_gate4,
            "conv_w": conv_w[l].astype(F32),
            "a_log": _pad_vec(a_log[l]), "dt_bias": _pad_vec(dt_bias[l]),
            "a_norm_g": a_norm_g[l].reshape(1, HEAD_DIM).astype(F32),
            "f_bias": jnp.pad(f_bias[l].reshape(1, B_HEADS).astype(F32),
                              ((0, 0), (LANE_F_B, LANES - LANE_F_B - B_HEADS))),
            "c_w2p": jnp.pad(_pad_heads_c(c_w2[l]),
                             ((LANE_LR_C, LANES - LANE_LR_C - C_RANK), (0, 0))).astype(BF16),
            "c_b2p": _pad_heads_c(c_b2[l].reshape(1, C_KEY)).astype(F32),
            "c_norm_g": c_norm_g[l].reshape(1, C_DV).astype(F32),
            "w_out": w_out_b,
            "ln1_g": ln1_g[l], "ln1_b": ln1_b[l], "ln2_g": ln2_g[l], "ln2_b": ln2_b[l],
            "w_gate": w_gate.astype(F32), "w_up": w_up.astype(F32), "w_down": w_down_b,
        }
        hp, hpb, stp, prompt_kv = _layer(
            hp, hpb, nb, tp,
            jnp.zeros((nb, SUBLANES, A_CONV_DIM), F32),
            jnp.zeros((nb, A_HEADS, HEAD_DIM, HEAD_DIM), F32),
            None,
            jnp.zeros((nb, C_HEADS, C_DV, C_DK_PAD), F32), w, l, depth, alpha, prompt_kv)
        p_states.append(stp)
        buf8 = jnp.pad(state_a_conv[l].astype(F32), ((0, 0), (SUBLANES - (CONV_W - 1), 0), (0, 0)))
        s_c0t = jnp.pad(jnp.swapaxes(state_c_rec[l].astype(F32), 2, 3),
                        ((0, 0), (0, 0), (0, 0), (0, C_DK_PAD - C_DK)))
        hs, hsb, sts, _ = _layer(
            hs, hsb, ns, ts, buf8, state_a_rec[l].astype(F32),
            (cache_k, cache_v, cache_b_logf[l]), s_c0t, w, l, depth, alpha)
        s_states.append(sts)

    dp, ds = x_prompt.dtype, x_sample.dtype
    stack = lambda states, i, dt: jnp.stack([s[i] for s in states], axis=0).astype(dt)
    prompt_k, prompt_v = (a.reshape(depth, nb, tp, B_HEADS, HEAD_DIM).astype(dp) for a in prompt_kv)
    return ((hp.reshape(nb, tp, D_MODEL).astype(dp), hs.reshape(ns, ts, D_MODEL).astype(ds))
            + (stack(p_states, 0, dp), stack(p_states, 1, dp), prompt_k, prompt_v,
               stack(p_states, 4, dp), stack(p_states, 5, dp))
            + tuple(stack(s_states, i, ds) for i in range(6)))
```

```python
import functools

import jax
import jax.numpy as jnp
from jax import lax
from jax.experimental import pallas as pl
from jax.experimental.pallas import tpu as pltpu

F32 = jnp.float32
BF16 = jnp.bfloat16

D_MODEL = 4096
GDN_CHUNK = 64
GLA_CHUNK = 128
HEAD_DIM = 128
A_HEADS = 12
A_KEY = A_HEADS * HEAD_DIM
A_VAL = A_HEADS * HEAD_DIM
A_CONV_DIM = 2 * A_KEY + A_VAL
CONV_W = 4
B_HEADS = 8
B_WIDTH = B_HEADS * HEAD_DIM
C_HEADS = 4
C_DV = 384
C_DK = 192
C_DK_PAD = 256
C_KEY = C_HEADS * C_DK
C_KEY_PAD = C_HEADS * C_DK_PAD
C_VAL = C_HEADS * C_DV
C_RANK = 16
C_TAU = 16.0
D_FF = 11008
DEPTH_ALPHA_POW = 0.25
LN_EPS = 1e-5
RMS_EPS = 1e-6

LANES = 128
SUBLANES = 8
VMEM_LIMIT = 56 * 1024 * 1024

COL_QKV_A = 0
COL_Z_A = 4608
COL_V_C = 6144
COL_R_C = 7680
COL_Q_B = 9216
COL_K_B = 10240
COL_V_B = 11264
COL_Q_C = 12288
COL_K_C = 13312
IN_BIG = 14336
IN_GATE = LANES
LANE_A_A = 0
LANE_B_A = LANE_A_A + A_HEADS
LANE_F_B = LANE_B_A + A_HEADS
LANE_LR_C = LANE_F_B + B_HEADS

NEG_BIG = -1e30
LOG2E = 1.4426950408889634

PROJ_TM = 1024
PROJ_TN = 1024
FFN_TM = 2048
FFN_TN = 256
DOWN_TM = 512
DOWN_TN = 512
LN_TM = 256
FOX_TQ = 512
FOX_PREP_TB = 256
FOX_HEADS_PER_STEP = 4
FGATE_TB = 2048
FGATE_SUB = 256
GDN_SEQS_PER_STEP = 1
GDN_HEADS_PER_GROUP = 6
GDN_GROUP_SKEW = 1


def _cparams(sem):
    return pltpu.CompilerParams(dimension_semantics=sem, vmem_limit_bytes=VMEM_LIMIT)


def _sigmoid(x):
    return 1.0 / (1.0 + jnp.exp(-x))


def _silu(x):
    return x * _sigmoid(x)


def _softplus(x):
    return jnp.maximum(x, 0.0) + jnp.log(1.0 + jnp.exp(-jnp.abs(x)))


def _log_sigmoid(x):
    return -_softplus(-x)


def _split3(x):
    x1 = x.astype(BF16)
    r1 = x - x1.astype(F32)
    x2 = r1.astype(BF16)
    x3 = (r1 - x2.astype(F32)).astype(BF16)
    return x1, x2, x3


def _dot(a, b):
    return jnp.dot(a, b, preferred_element_type=F32)


def _dot_nt(a, b):
    return lax.dot_general(a, b, (((1,), (1,)), ((), ())), preferred_element_type=F32)


def _dot_tn(a, b):
    return lax.dot_general(a, b, (((0,), (0,)), ((), ())), preferred_element_type=F32)


def _dot_exact_lhs(a_bf16, x):
    x1, x2, x3 = _split3(x)
    return _dot(a_bf16, x1) + _dot(a_bf16, x2) + _dot(a_bf16, x3)


def _solve_dot(a, b):
    return _dot(a.astype(BF16), b.astype(BF16))


def _iota2(shape, dim):
    return lax.broadcasted_iota(jnp.int32, shape, dim)


def _tri_incl(n):
    return (_iota2((n, n), 1) <= _iota2((n, n), 0)).astype(BF16)


def _mm_kernel(x_ref, w_ref, o_ref):
    o_ref[...] = _dot(x_ref[...], w_ref[...]).astype(o_ref.dtype)


def _matmul(x, w, layer, tm, tn, out_dtype):
    m, k = x.shape
    n = w.shape[2]
    tm = min(tm, m)
    return pl.pallas_call(
        _mm_kernel,
        grid=(m // tm, n // tn),
        in_specs=[pl.BlockSpec((tm, k), lambda i, j: (i, 0)),
                  pl.BlockSpec((None, k, tn), lambda i, j: (layer, 0, j))],
        out_specs=pl.BlockSpec((tm, tn), lambda i, j: (i, j)),
        out_shape=jax.ShapeDtypeStruct((m, n), out_dtype),
        compiler_params=_cparams(("parallel", "parallel")),
    )(x, w)


def _lane_partial(v):
    part = v[:, 0:LANES]
    for c in range(1, v.shape[1] // LANES):
        part = part + v[:, c * LANES:(c + 1) * LANES]
    return part


def _residual_epilogue(alpha, r, acc, v_ref, s1_ref, s2_ref):
    v = alpha * r + acc
    v_ref[...] = v
    p1 = _lane_partial(v)
    p2 = _lane_partial(v * v)
    first = pl.program_id(1) == 0
    s1_ref[...] = jnp.where(first, p1, s1_ref[...] + p1)
    s2_ref[...] = jnp.where(first, p2, s2_ref[...] + p2)


def _row_stats(s1, s2, width):
    mu = jnp.sum(s1, axis=-1, keepdims=True) * (1.0 / width)
    var = jnp.sum(s2, axis=-1, keepdims=True) * (1.0 / width) - mu * mu
    return mu, lax.rsqrt(var + LN_EPS)


def _outproj_kernel(alpha, a_ref, b_ref, c_ref, w_ref, r_ref, v_ref, s1_ref, s2_ref):
    acc = (_dot(a_ref[...], w_ref[0:A_VAL, :])
           + _dot(b_ref[...], w_ref[A_VAL:A_VAL + B_WIDTH, :])
           + _dot(c_ref[...], w_ref[A_VAL + B_WIDTH:D_MODEL, :]))
    _residual_epilogue(alpha, r_ref[...], acc, v_ref, s1_ref, s2_ref)


def _resid_out(m, tm, tn):
    tile = pl.BlockSpec((tm, tn), lambda i, j: (i, j))
    stat = pl.BlockSpec((tm, LANES), lambda i, j: (i, 0))
    return ([tile, stat, stat],
            [jax.ShapeDtypeStruct((m, D_MODEL), F32)] + [jax.ShapeDtypeStruct((m, LANES), F32)] * 2)


def _outproj(oa, ob, oc, w, layer, resid, alpha, tm, tn):
    m = oa.shape[0]
    tm = min(tm, m)
    rows = lambda width: pl.BlockSpec((tm, width), lambda i, j: (i, 0))
    out_specs, out_shape = _resid_out(m, tm, tn)
    return pl.pallas_call(
        functools.partial(_outproj_kernel, alpha),
        grid=(m // tm, D_MODEL // tn),
        in_specs=[rows(A_VAL), rows(B_WIDTH), rows(C_VAL),
                  pl.BlockSpec((None, D_MODEL, tn), lambda i, j: (layer, 0, j)),
                  pl.BlockSpec((tm, tn), lambda i, j: (i, j))],
        out_specs=out_specs, out_shape=out_shape,
        compiler_params=_cparams(("parallel", "arbitrary")),
    )(oa, ob, oc, w, resid)


def _down_kernel(alpha, h_ref, w_ref, v1_ref, s1_ref, s2_ref, g_ref, b_ref, v_ref, t1_ref, t2_ref):
    acc = _dot(h_ref[...], w_ref[...])
    mu, rstd = _row_stats(s1_ref[...], s2_ref[...], D_MODEL)
    x1 = (v1_ref[...] - mu) * rstd * g_ref[...] + b_ref[...]
    _residual_epilogue(alpha, x1, acc, v_ref, t1_ref, t2_ref)


def _down_proj(hmid, w, layer, v1, s1, s2, g, b, alpha, tm, tn):
    m, k = hmid.shape
    tm = min(tm, m)
    tile = pl.BlockSpec((tm, tn), lambda i, j: (i, j))
    stat = pl.BlockSpec((tm, LANES), lambda i, j: (i, 0))
    vec = pl.BlockSpec((1, tn), lambda i, j: (0, j))
    out_specs, out_shape = _resid_out(m, tm, tn)
    return pl.pallas_call(
        functools.partial(_down_kernel, alpha),
        grid=(m // tm, D_MODEL // tn),
        in_specs=[pl.BlockSpec((tm, k), lambda i, j: (i, 0)),
                  pl.BlockSpec((None, k, tn), lambda i, j: (layer, 0, j)),
                  tile, stat, stat, vec, vec],
        out_specs=out_specs, out_shape=out_shape,
        compiler_params=_cparams(("parallel", "arbitrary")),
    )(hmid, w, v1, s1, s2, g.reshape(1, D_MODEL), b.reshape(1, D_MODEL))


def _swiglu_kernel(x_ref, wg_ref, wu_ref, o_ref):
    x = x_ref[...]
    a = _dot(x, wg_ref[...].astype(BF16))
    b = _dot(x, wu_ref[...].astype(BF16))
    o_ref[...] = (_silu(a) * b).astype(o_ref.dtype)


def _swiglu(x, wg, wu, layer, tm, tn):
    m, k = x.shape
    n = wg.shape[2]
    tm = min(tm, m)
    wspec = pl.BlockSpec((None, k, tn), lambda i, j: (layer, 0, j))
    return pl.pallas_call(
        _swiglu_kernel,
        grid=(m // tm, n // tn),
        in_specs=[pl.BlockSpec((tm, k), lambda i, j: (i, 0), pipeline_mode=pl.Buffered(1)), wspec, wspec],
        out_specs=pl.BlockSpec((tm, tn), lambda i, j: (i, j)),
        out_shape=jax.ShapeDtypeStruct((m, n), BF16),
        compiler_params=_cparams(("parallel", "parallel")),
    )(x, wg, wu)


def _ln_apply_kernel(with_f32, v_ref, s1_ref, s2_ref, g_ref, b_ref, *out_refs):
    mu, rstd = _row_stats(s1_ref[...], s2_ref[...], D_MODEL)
    out = (v_ref[...] - mu) * rstd * g_ref[...] + b_ref[...]
    out_refs[0][...] = out.astype(BF16)
    if with_f32:
        out_refs[1][...] = out


def _ln_apply(v, s1, s2, g, b, with_f32, tm=LN_TM):
    m, d = v.shape
    tm = min(tm, m)
    row = pl.BlockSpec((tm, d), lambda i: (i, 0))
    stat = pl.BlockSpec((tm, LANES), lambda i: (i, 0))
    vec = pl.BlockSpec((1, d), lambda i: (0, 0))
    n_out = 2 if with_f32 else 1
    return pl.pallas_call(
        functools.partial(_ln_apply_kernel, with_f32),
        grid=(m // tm,),
        in_specs=[row, stat, stat, vec, vec],
        out_specs=[row] * n_out,
        out_shape=[jax.ShapeDtypeStruct((m, d), BF16), jax.ShapeDtypeStruct((m, d), F32)][:n_out],
        compiler_params=_cparams(("parallel",)),
    )(v, s1, s2, g.reshape(1, d), b.reshape(1, d))


def _fgate_kernel(apply_gate, tb, x_ref, bias_ref, c0_ref, lf_ref, c_ref, carry):
    @pl.when(pl.program_id(1) == 0)
    def _():
        carry[...] = c0_ref[0]

    x = x_ref[0]
    lf = _log_sigmoid(x + bias_ref[...]) if apply_gate else x
    lf_ref[0] = lf
    sub = min(tb, FGATE_SUB)
    tri = _tri_incl(sub)
    parts = [_dot_exact_lhs(tri, lf[k * sub:(k + 1) * sub]) for k in range(tb // sub)]
    run = carry[...]
    for k, part in enumerate(parts):
        c_ref[0, k * sub:(k + 1) * sub, :] = part + run
        run = run + part[sub - 1:sub, :]
    carry[...] = run


def _fgate(x, bias, c0, apply_gate, tb):
    n, t, _ = x.shape
    tb = min(tb, t)
    blk = pl.BlockSpec((1, tb, LANES), lambda i, j: (i, j, 0))
    return pl.pallas_call(
        functools.partial(_fgate_kernel, apply_gate, tb),
        grid=(n, t // tb),
        in_specs=[pl.BlockSpec((1, tb, LANES), lambda i, j: (i, j, 0)),
                  pl.BlockSpec((1, LANES), lambda i, j: (0, 0)),
                  pl.BlockSpec((1, 1, LANES), lambda i, j: (i, 0, 0))],
        out_specs=[blk, blk],
        out_shape=[jax.ShapeDtypeStruct((n, t, LANES), F32)] * 2,
        scratch_shapes=[pltpu.VMEM((1, LANES), F32)],
        compiler_params=_cparams(("parallel", "arbitrary")),
    )(x, bias, c0)


def _gdn_kernel(L, NB, qkv_ref, z_ref, gate_ref, buf_ref, convw_ref, alog_ref, dtb_ref, ng_ref, s0_ref,
                o_ref, s_ref, xs):
    @pl.when(pl.program_id(1) == 0)
    def _():
        xs[:, 0:SUBLANES, :] = buf_ref[...]
        s_ref[...] = s0_ref[...]

    xs[:, SUBLANES:SUBLANES + L, :] = qkv_ref[...]

    def conv_cols(b, c0):
        r0 = SUBLANES - (CONV_W - 1)
        acc = xs[b, r0:r0 + L, c0:c0 + HEAD_DIM] * convw_ref[0:1, c0:c0 + HEAD_DIM]
        for i in range(1, CONV_W):
            acc = acc + xs[b, r0 + i:r0 + i + L, c0:c0 + HEAD_DIM] * convw_ref[i:i + 1, c0:c0 + HEAD_DIM]
        return _silu(acc)

    def l2n(x):
        return x * lax.rsqrt(jnp.sum(x * x, axis=-1, keepdims=True) + RMS_EPS)

    tri = _tri_incl(L)
    eye_l = (_iota2((LANES, LANES), 0) == _iota2((LANES, LANES), 1)).astype(BF16)
    seqs = []
    for b in range(NB):
        a_in = gate_ref[b]
        g = -jnp.exp(alog_ref[...]) * _softplus(a_in + dtb_ref[...])
        gc = _dot_exact_lhs(tri, g)
        g1, g2, g3 = _split3(gc)
        gct = _dot_nt(eye_l, g1) + _dot_nt(eye_l, g2) + _dot_nt(eye_l, g3)
        seqs.append(dict(gc=gc, gct=gct, beta=_sigmoid(a_in)))

    row = _iota2((L, L), 0)
    col = _iota2((L, L), 1)
    eye = (row == col).astype(F32)
    n_lvl = L.bit_length() - 2

    mm = _solve_dot

    def st_prep(hd):
        b, h = hd["b"], hd["h"]
        sq = seqs[b]
        q = l2n(conv_cols(b, h * HEAD_DIM)) * (HEAD_DIM ** -0.5)
        k = l2n(conv_cols(b, A_KEY + h * HEAD_DIM))
        v = conv_cols(b, 2 * A_KEY + h * HEAD_DIM)
        gcol = sq["gc"][:, h:h + 1]
        bcol = sq["beta"][:, LANE_B_A + h:LANE_B_A + h + 1]
        glast = sq["gc"][L - 1:L, h:h + 1]
        dec = jnp.exp(jnp.where(row >= col, gcol - sq["gct"][h:h + 1, :], NEG_BIG))
        eg = jnp.exp(gcol)
        qb = q.astype(BF16)
        kb = k.astype(BF16)
        p = -(bcol * _dot_nt(kb, kb) * jnp.where(row > col, dec, 0.0))
        hd.update(qb=qb, p=p, t=eye + p, eg=eg, e_last=jnp.exp(glast),
                  qkd=(_dot_nt(qb, kb) * dec).astype(BF16),
                  rhs=jnp.concatenate([(bcol * eg) * k, bcol * v], axis=1),
                  k_end=(k * jnp.exp(glast - gcol)).astype(BF16),
                  s=s_ref[b, h])

    def st_square(hd):
        hd["p"] = mm(hd["p"], hd["p"])

    def st_level(hd):
        pt = mm(hd["p"], jnp.concatenate([hd["p"], hd["t"]], axis=1))
        hd["t"] = hd["t"] + pt[:, L:2 * L]
        hd["p"] = pt[:, 0:L]

    def st_last_level(hd):
        hd["t"] = hd["t"] + mm(hd["p"], hd["t"])

    def st_solve(hd):
        hd["tr"] = mm(hd["t"], hd["rhs"])

    def st_state(hd):
        tr = hd["tr"]
        sb = hd["s"].astype(BF16)
        hd["ub"] = (tr[:, HEAD_DIM:2 * HEAD_DIM] - _dot(tr[:, 0:HEAD_DIM].astype(BF16), sb)).astype(BF16)
        hd["qs"] = _dot(hd["qb"], sb)

    def st_out(hd):
        b, h = hd["b"], hd["h"]
        o = hd["eg"] * hd["qs"] + _dot(hd["qkd"], hd["ub"])
        s_ref[b, h] = hd["e_last"] * hd["s"] + _dot_tn(hd["k_end"], hd["ub"])
        o = o * lax.rsqrt(jnp.mean(o * o, axis=-1, keepdims=True) + RMS_EPS) * ng_ref[...]
        zh = z_ref[b, :, h * HEAD_DIM:(h + 1) * HEAD_DIM]
        o_ref[b, :, h * HEAD_DIM:(h + 1) * HEAD_DIM] = (o * _silu(zh)).astype(BF16)
        hd.clear()

    stages = [st_prep, st_square] + [st_level] * (n_lvl - 1) + [st_last_level, st_solve, st_state, st_out]
    heads = [dict(b=b, h=h) for b in range(NB) for h in range(A_HEADS)]
    groups = [heads[i:i + GDN_HEADS_PER_GROUP] for i in range(0, len(heads), GDN_HEADS_PER_GROUP)]
    skew = GDN_GROUP_SKEW
    for wave in range(skew * (len(groups) - 1) + len(stages)):
        for g, group in enumerate(groups):
            if 0 <= wave - skew * g < len(stages):
                for hd in group:
                    stages[wave - skew * g](hd)

    tail = xs[:, L:L + SUBLANES, :]
    xs[:, 0:SUBLANES, :] = tail


def _gdn(hbig, hgate, conv_buf8, conv_w, a_log, dt_bias, norm_g, s0):
    n, t, _ = hbig.shape
    L = min(t, GDN_CHUNK)
    nb = GDN_SEQS_PER_STEP
    const2 = lambda i, j: (0, 0)
    state = pl.BlockSpec((nb, A_HEADS, HEAD_DIM, HEAD_DIM), lambda i, j: (i, 0, 0, 0))
    return pl.pallas_call(
        functools.partial(_gdn_kernel, L, nb),
        grid=(n // nb, t // L),
        in_specs=[pl.BlockSpec((nb, L, A_CONV_DIM), lambda i, j: (i, j, COL_QKV_A // A_CONV_DIM)),
                  pl.BlockSpec((nb, L, A_VAL), lambda i, j: (i, j, COL_Z_A // A_VAL)),
                  pl.BlockSpec((nb, L, LANES), lambda i, j: (i, j, 0)),
                  pl.BlockSpec((nb, SUBLANES, A_CONV_DIM), lambda i, j: (i, 0, 0)),
                  pl.BlockSpec((CONV_W, A_CONV_DIM), const2),
                  pl.BlockSpec((1, LANES), const2),
                  pl.BlockSpec((1, LANES), const2),
                  pl.BlockSpec((1, HEAD_DIM), const2),
                  state],
        out_specs=[pl.BlockSpec((nb, L, A_VAL), lambda i, j: (i, j, 0)), state],
        out_shape=[jax.ShapeDtypeStruct((n, t, A_VAL), BF16),
                   jax.ShapeDtypeStruct((n, A_HEADS, HEAD_DIM, HEAD_DIM), F32)],
        scratch_shapes=[pltpu.VMEM((nb, L + SUBLANES, A_CONV_DIM), F32)],
        compiler_params=_cparams(("parallel", "arbitrary")),
    )(hbig, hbig, hgate, conv_buf8, conv_w, a_log, dt_bias, norm_g, s0)


def _gla_kernel(L, q_ref, k_ref, v_ref, r_ref, gate_ref, w2_ref, b2_ref, ng_ref, s0_ref,
                o_ref, s_ref):
    @pl.when(pl.program_id(1) == 0)
    def _():
        s_ref[0] = s0_ref[0]

    z = _dot(gate_ref[0].astype(BF16), w2_ref[...]) + b2_ref[...]
    lg = _log_sigmoid(z) * (1.0 / C_TAU)
    b = _dot_exact_lhs(_tri_incl(L), lg)
    causal = _iota2((L, L), 0) >= _iota2((L, L), 1)

    heads = []
    for h in range(C_HEADS):
        ks = slice(h * C_DK_PAD, (h + 1) * C_DK_PAD)
        vs = slice(h * C_DV, (h + 1) * C_DV)
        bh = b[:, ks]
        blast = bh[L - 1:L, :]
        bref = bh[L // 2:L // 2 + 1, :]
        q = q_ref[0, :, ks] * (C_DK ** -0.5)
        k = k_ref[0, :, ks]
        heads.append(dict(
            vs=vs, vb=v_ref[0, :, vs].astype(BF16), st=s_ref[0, h],
            q_state=(q * jnp.exp(bh)).astype(BF16),
            q_in=(q * jnp.exp(bh - bref)).astype(BF16),
            k_in=(k * jnp.exp(bref - bh)).astype(BF16),
            k_end=(k * jnp.exp(blast - bh)).astype(BF16),
            e_last=jnp.exp(blast)))
    for hd in heads:
        hd["att"] = jnp.where(causal, _dot_nt(hd["q_in"], hd["k_in"]), 0.0).astype(BF16)
        hd["o"] = _dot_nt(hd["q_state"], hd["st"].astype(BF16))
        hd["st_new"] = hd["st"] * hd["e_last"] + _dot_tn(hd["vb"], hd["k_end"])
    for hd in heads:
        o = hd["o"] + _dot(hd["att"], hd["vb"])
        hd["o"] = o * lax.rsqrt(jnp.mean(o * o, axis=-1, keepdims=True) + RMS_EPS) * ng_ref[...]
    for h, hd in enumerate(heads):
        s_ref[0, h] = hd["st_new"]
        o_ref[0, :, hd["vs"]] = (hd["o"] * _silu(r_ref[0, :, hd["vs"]])).astype(BF16)


def _gla(hbig, hgate, w2p, b2p, norm_g, s0t):
    n, t, _ = hbig.shape
    L = min(t, GLA_CHUNK)
    const2 = lambda i, j: (0, 0)
    state = pl.BlockSpec((1, C_HEADS, C_DV, C_DK_PAD), lambda i, j: (i, 0, 0, 0))
    return pl.pallas_call(
        functools.partial(_gla_kernel, L),
        grid=(n, t // L),
        in_specs=[pl.BlockSpec((1, L, C_KEY_PAD), lambda i, j: (i, j, COL_Q_C // C_KEY_PAD)),
                  pl.BlockSpec((1, L, C_KEY_PAD), lambda i, j: (i, j, COL_K_C // C_KEY_PAD)),
                  pl.BlockSpec((1, L, C_VAL), lambda i, j: (i, j, COL_V_C // C_VAL)),
                  pl.BlockSpec((1, L, C_VAL), lambda i, j: (i, j, COL_R_C // C_VAL)),
                  pl.BlockSpec((1, L, LANES), lambda i, j: (i, j, 0)),
                  pl.BlockSpec((LANES, C_KEY_PAD), const2),
                  pl.BlockSpec((1, C_KEY_PAD), const2),
                  pl.BlockSpec((1, C_DV), const2),
                  state],
        out_specs=[pl.BlockSpec((1, L, C_VAL), lambda i, j: (i, j, 0)), state],
        out_shape=[jax.ShapeDtypeStruct((n, t, C_VAL), BF16),
                   jax.ShapeDtypeStruct((n, C_HEADS, C_DV, C_DK_PAD), F32)],
        compiler_params=_cparams(("parallel", "arbitrary")),
    )(hbig, hbig, hbig, hbig, hgate, w2p, b2p, norm_g, s0t)


def _fox_prep_kernel(q_ref, k_ref, v_ref, c_ref, qp_ref, kp_ref, vp_ref):
    tb = q_ref.shape[1]
    lane = _iota2((tb, LANES), 1)
    ones_q = jnp.where((lane >= 3) & (lane < 6), 1.0, 0.0)
    ones_k = jnp.where(lane < 3, 1.0, 0.0)
    c2 = c_ref[0] * LOG2E
    for h in range(B_HEADS):
        hs = slice(h * HEAD_DIM, (h + 1) * HEAD_DIM)
        c1, cm, cl = (t.astype(F32) for t in _split3(c2[:, LANE_F_B + h:LANE_F_B + h + 1]))
        ext_q = jnp.where(lane == 0, c1, jnp.where(lane == 1, cm, jnp.where(lane == 2, cl, ones_q)))
        ext_k = jnp.where(lane == 3, -c1, jnp.where(lane == 4, -cm, jnp.where(lane == 5, -cl, ones_k)))
        base = 2 * h * HEAD_DIM
        qp_ref[0, :, base:base + HEAD_DIM] = (q_ref[0, :, hs] * (HEAD_DIM ** -0.5 * LOG2E)).astype(BF16)
        qp_ref[0, :, base + HEAD_DIM:base + 2 * HEAD_DIM] = ext_q.astype(BF16)
        kp_ref[0, :, base:base + HEAD_DIM] = k_ref[0, :, hs].astype(BF16)
        kp_ref[0, :, base + HEAD_DIM:base + 2 * HEAD_DIM] = ext_k.astype(BF16)
    vp_ref[0] = v_ref[0].astype(BF16)


def _fox_prep_cache_kernel(q_ref, k_ref, v_ref, c_ref, *rest):
    qp_ref, kp_ref, vp_ref, kc_ref, vc_ref = rest[-5:]
    _fox_prep_kernel(q_ref, k_ref, v_ref, c_ref, qp_ref, kp_ref, vp_ref)
    kc_ref[0] = k_ref[0]
    vc_ref[0] = v_ref[0]


def _fox_prep(hbig, c, layer, depth, caches, tb=FOX_PREP_TB):
    n, t, _ = hbig.shape
    tb = min(tb, t)
    col = lambda blk: (lambda i, j: (i, j, blk))
    wide = pl.BlockSpec((1, tb, 2 * B_WIDTH), col(0))
    cache_spec = pl.BlockSpec((None, 1, tb, B_WIDTH), lambda i, j: (layer, i, j, 0))
    cache_shape = jax.ShapeDtypeStruct((depth, n, t, B_WIDTH), F32)
    in_specs = [pl.BlockSpec((1, tb, B_WIDTH), col(COL_Q_B // B_WIDTH)),
                pl.BlockSpec((1, tb, B_WIDTH), col(COL_K_B // B_WIDTH)),
                pl.BlockSpec((1, tb, B_WIDTH), col(COL_V_B // B_WIDTH)),
                pl.BlockSpec((1, tb, LANES), col(0))]
    args = [hbig, hbig, hbig, c]
    aliases = {}
    if caches is not None:
        in_specs += [pl.BlockSpec(memory_space=pl.ANY)] * 2
        args += list(caches)
        aliases = {4: 3, 5: 4}
    return pl.pallas_call(
        _fox_prep_cache_kernel,
        grid=(n, t // tb),
        in_specs=in_specs,
        out_specs=[wide, wide, pl.BlockSpec((1, tb, B_WIDTH), col(0)), cache_spec, cache_spec],
        out_shape=[jax.ShapeDtypeStruct((n, t, 2 * B_WIDTH), BF16),
                   jax.ShapeDtypeStruct((n, t, 2 * B_WIDTH), BF16),
                   jax.ShapeDtypeStruct((n, t, B_WIDTH), BF16),
                   cache_shape, cache_shape],
        input_output_aliases=aliases,
        compiler_params=_cparams(("parallel", "parallel")),
    )(*args)


def _fox_prompt_kernel(tq, q_ref, k_ref, v_ref, o_ref, m_sc, l_sc, acc_sc):
    qi = pl.program_id(2)
    m_sc[...] = jnp.full(m_sc.shape, NEG_BIG, F32)
    l_sc[...] = jnp.zeros(l_sc.shape, F32)
    acc_sc[...] = jnp.zeros(acc_sc.shape, F32)
    n_tiles = tq // LANES
    heads = range(FOX_HEADS_PER_STEP)
    qw = 2 * HEAD_DIM

    def block(j, masked):
        off = pl.multiple_of(j * tq, tq)
        ss = [_dot_nt(q_ref[0, :, h * qw:(h + 1) * qw], k_ref[0, pl.ds(off, tq), h * qw:(h + 1) * qw])
              for h in heads]
        if masked:
            visible = _iota2((tq, tq), 1) <= _iota2((tq, tq), 0)
            ss = [jnp.where(visible, s, NEG_BIG) for s in ss]
        new = []
        for h, s in zip(heads, ss):
            tiles = [s[:, c * LANES:(c + 1) * LANES] for c in range(n_tiles)]
            mx = tiles[0]
            for tl in tiles[1:]:
                mx = jnp.maximum(mx, tl)
            m_prev = m_sc[h]
            m_new = jnp.maximum(m_prev, jnp.max(mx, axis=-1, keepdims=True))
            alpha = jnp.exp2(m_prev - m_new)
            ps = [jnp.exp2(tl - m_new) for tl in tiles]
            psum = ps[0]
            for pt in ps[1:]:
                psum = psum + pt
            p = jnp.concatenate([pt.astype(BF16) for pt in ps], axis=1)
            pv = _dot(p, v_ref[0, pl.ds(off, tq), h * HEAD_DIM:(h + 1) * HEAD_DIM])
            new.append((m_new, alpha * l_sc[h] + psum, alpha * acc_sc[h] + pv))
        for h, (m_new, l_new, acc_new) in zip(heads, new):
            m_sc[h] = m_new
            l_sc[h] = l_new
            acc_sc[h] = acc_new

    def body(j, carry):
        block(j, False)
        return carry

    lax.fori_loop(0, qi, body, 0)
    block(qi, True)
    for h in heads:
        o_ref[0, :, h * HEAD_DIM:(h + 1) * HEAD_DIM] = (
            acc_sc[h] / jnp.sum(l_sc[h], axis=-1, keepdims=True)).astype(BF16)


def _fox_prompt(qp, kp, vp, tq=FOX_TQ):
    n, t, _ = vp.shape
    tq = min(tq, t)
    hps = FOX_HEADS_PER_STEP
    return pl.pallas_call(
        functools.partial(_fox_prompt_kernel, tq),
        grid=(n, B_HEADS // hps, t // tq),
        in_specs=[pl.BlockSpec((1, tq, hps * 2 * HEAD_DIM), lambda i, h, qi: (i, qi, h)),
                  pl.BlockSpec((1, t, hps * 2 * HEAD_DIM), lambda i, h, qi: (i, 0, h),
                               pipeline_mode=pl.Buffered(1)),
                  pl.BlockSpec((1, t, hps * HEAD_DIM), lambda i, h, qi: (i, 0, h),
                               pipeline_mode=pl.Buffered(1))],
        out_specs=pl.BlockSpec((1, tq, hps * HEAD_DIM), lambda i, h, qi: (i, qi, h)),
        out_shape=jax.ShapeDtypeStruct((n, t, B_WIDTH), BF16),
        scratch_shapes=[pltpu.VMEM((hps, tq, LANES), F32),
                        pltpu.VMEM((hps, tq, LANES), F32),
                        pltpu.VMEM((hps, tq, HEAD_DIM), F32)],
        compiler_params=_cparams(("parallel", "parallel", "arbitrary")),
    )(qp, kp, vp)


def _fox_sample_kernel(t, p, q_ref, kn_ref, vn_ref, kp_ref, vp_ref, cq_ref, ckp_ref, ckn_ref, o_ref):
    causal = _iota2((t, t), 1) <= _iota2((t, t), 0)
    for h in range(B_HEADS):
        hs = slice(h * HEAD_DIM, (h + 1) * HEAD_DIM)
        kp = kp_ref[0, pl.ds(h, p, stride=B_HEADS), :].astype(BF16)
        vp = vp_ref[0, pl.ds(h, p, stride=B_HEADS), :].astype(BF16)
        qb = (q_ref[0, :, hs] * (HEAD_DIM ** -0.5)).astype(BF16)
        cq = cq_ref[0, h]
        sp = _dot_nt(qb, kp) + (cq - ckp_ref[0, h])
        sn = _dot_nt(qb, kn_ref[0, :, hs].astype(BF16)) + (cq - ckn_ref[0, h])
        sn = jnp.where(causal, sn, NEG_BIG)
        m = jnp.maximum(jnp.max(sp, axis=-1, keepdims=True), jnp.max(sn, axis=-1, keepdims=True))
        pp = jnp.exp(sp - m)
        pn = jnp.exp(sn - m)
        den = jnp.sum(pp, axis=-1, keepdims=True) + jnp.sum(pn, axis=-1, keepdims=True)
        o = _dot(pp.astype(BF16), vp) + _dot(pn.astype(BF16), vn_ref[0, :, hs].astype(BF16))
        o_ref[0, :, hs] = (o / den).astype(BF16)


def _fox_sample(hbig, k_past, v_past, layer, cq, ckp, ckn):
    n, t, _ = hbig.shape
    rows = k_past.shape[2]
    p = rows // B_HEADS
    col = lambda base: (lambda i: (i, 0, base // B_WIDTH))
    past = pl.BlockSpec((None, 1, rows, HEAD_DIM), lambda i: (layer, i, 0, 0))
    whole = lambda a: pl.BlockSpec((1,) + a.shape[1:], lambda i: (i, 0, 0, 0))
    return pl.pallas_call(
        functools.partial(_fox_sample_kernel, t, p),
        grid=(n,),
        in_specs=[pl.BlockSpec((1, t, B_WIDTH), col(COL_Q_B)),
                  pl.BlockSpec((1, t, B_WIDTH), col(COL_K_B)),
                  pl.BlockSpec((1, t, B_WIDTH), col(COL_V_B)),
                  past, past, whole(cq), whole(ckp), whole(ckn)],
        out_specs=pl.BlockSpec((1, t, B_WIDTH), lambda i: (i, 0, 0)),
        out_shape=jax.ShapeDtypeStruct((n, t, B_WIDTH), BF16),
        compiler_params=_cparams(("parallel",)),
    )(hbig, hbig, hbig, k_past, v_past, cq, ckp, ckn)


def _pad_cols(w, width):
    return jnp.pad(w, [(0, 0)] * (w.ndim - 1) + [(0, width - w.shape[-1])])


def _pad_heads_c(w):
    lead = w.shape[:-1]
    w = w.reshape(lead + (C_HEADS, C_DK))
    w = jnp.pad(w, [(0, 0)] * len(lead) + [(0, 0), (0, C_DK_PAD - C_DK)])
    return w.reshape(lead + (C_KEY_PAD,))


def _prep_w_in(w):
    o = 0
    parts = {}
    for name, size in (("qkv_a", A_CONV_DIM), ("z_a", A_VAL), ("a_a", A_HEADS), ("b_a", A_HEADS),
                       ("q_b", B_WIDTH), ("k_b", B_WIDTH), ("v_b", B_WIDTH), ("f_b", B_HEADS),
                       ("q_c", C_KEY), ("k_c", C_KEY), ("v_c", C_VAL), ("r_c", C_VAL), ("lr_c", C_RANK)):
        parts[name] = w[..., o:o + size].astype(BF16)
        o += size
    big = jnp.concatenate([parts["qkv_a"], parts["z_a"], parts["v_c"], parts["r_c"],
                           parts["q_b"], parts["k_b"], parts["v_b"],
                           _pad_heads_c(parts["q_c"]), _pad_heads_c(parts["k_c"])], axis=-1)
    gate = _pad_cols(jnp.concatenate([parts[nm] for nm in ("a_a", "b_a", "f_b", "lr_c")], axis=-1), LANES)
    return big, gate


def _pad_vec(v):
    return _pad_cols(v.reshape(1, -1).astype(F32), LANES)


def _mixer(x, xb, n, t, conv_buf8, s_a0, past_b, s_c0t, w, layer, depth, alpha, kv_caches):
    m = n * t
    hbig = _matmul(xb, w["w_big"], layer, PROJ_TM, PROJ_TN, F32).reshape(n, t, IN_BIG)
    hgate = _matmul(xb, w["w_gate4"], layer, PROJ_TM, IN_GATE, F32).reshape(n, t, IN_GATE)

    oa, s_a = _gdn(hbig, hgate, conv_buf8, w["conv_w"], w["a_log"], w["dt_bias"], w["a_norm_g"], s_a0)
    oc, s_ct = _gla(hbig, hgate, w["c_w2p"], w["c_b2p"], w["c_norm_g"], s_c0t)

    zero_c = jnp.zeros((n, 1, LANES), F32)
    if past_b is None:
        lf, c = _fgate(hgate, w["f_bias"], zero_c, True, FGATE_TB)
        qp, kp, vp, k_cache, v_cache = _fox_prep(hbig, c, layer, depth, kv_caches)
        kv_caches = (k_cache, v_cache)
        ob = _fox_prompt(qp, kp, vp)
        kb = vb = None
    else:
        k_past, v_past, lf_past = past_b
        p = lf_past.shape[1]
        lf_past = jnp.pad(lf_past.astype(F32), ((0, 0), (0, 0), (LANE_F_B, LANES - LANE_F_B - B_HEADS)))
        _, c_past = _fgate(lf_past, w["f_bias"], zero_c, False, FGATE_TB)
        lf, c = _fgate(hgate, w["f_bias"], c_past[:, p - 1:p, :], True, t)
        fb = slice(LANE_F_B, LANE_F_B + B_HEADS)
        cq = jnp.swapaxes(c[:, :, fb], 1, 2)[..., None]
        ckn = jnp.swapaxes(c[:, :, fb], 1, 2)[:, :, None, :]
        ckp = jnp.swapaxes(c_past[:, :, fb], 1, 2)[:, :, None, :]
        ob = _fox_sample(hbig, k_past, v_past, layer, cq, ckp, ckn)

    y = _outproj(oa.reshape(m, A_VAL), ob.reshape(m, B_WIDTH), oc.reshape(m, C_VAL),
                 w["w_out"], layer, x, alpha, PROJ_TM, PROJ_TN)

    conv_new = hbig[:, t - (CONV_W - 1):, COL_QKV_A:COL_QKV_A + A_CONV_DIM]
    if past_b is not None:
        kb = hbig[:, :, COL_K_B:COL_K_B + B_WIDTH].reshape(n, t, B_HEADS, HEAD_DIM)
        vb = hbig[:, :, COL_V_B:COL_V_B + B_WIDTH].reshape(n, t, B_HEADS, HEAD_DIM)
    s_c = jnp.swapaxes(s_ct, 2, 3)[:, :, :C_DK, :]
    return y, (conv_new, s_a, kb, vb, lf[:, :, LANE_F_B:LANE_F_B + B_HEADS], s_c), kv_caches


def _layer(x, xb, n, t, conv_buf8, s_a0, past_b, s_c0t, w, layer, depth, alpha, kv_caches=None):
    (v1, s1, s2), st, kv_caches = _mixer(x, xb, n, t, conv_buf8, s_a0, past_b, s_c0t, w, layer, depth,
                                         alpha, kv_caches)
    x1b, = _ln_apply(v1, s1, s2, w["ln1_g"], w["ln1_b"], False)
    hmid = _swiglu(x1b, w["w_gate"], w["w_up"], layer, FFN_TM, FFN_TN)
    v2, t1, t2 = _down_proj(hmid, w["w_down"], layer, v1, s1, s2, w["ln1_g"], w["ln1_b"], alpha,
                            DOWN_TM, DOWN_TN)
    xb, x = _ln_apply(v2, t1, t2, w["ln2_g"], w["ln2_b"], True)
    return x, xb, st, kv_caches


def kernel(x_prompt, x_sample, state_a_conv, state_a_rec, cache_b_k, cache_b_v, cache_b_logf, state_c_rec, w_in, conv_w, a_log, dt_bias, a_norm_g, f_bias, c_w2, c_b2, c_norm_g, w_out, ln1_g, ln1_b, w_gate, w_up, w_down, ln2_g, ln2_b):
    depth = w_in.shape[0]
    alpha = (2 * depth) ** DEPTH_ALPHA_POW
    nb, tp, _ = x_prompt.shape
    ns, ts, _ = x_sample.shape

    hp = x_prompt.reshape(nb * tp, D_MODEL).astype(F32)
    hs = x_sample.reshape(ns * ts, D_MODEL).astype(F32)
    hpb = hp.astype(BF16)
    hsb = hs.astype(BF16)
    p_states, s_states = [], []
    w_big, w_gate4 = _prep_w_in(w_in)
    w_out_b, w_down_b = w_out.astype(BF16), w_down.astype(BF16)
    past_len = cache_b_k.shape[2]
    cache_k = cache_b_k.reshape(depth, ns, past_len * B_HEADS, HEAD_DIM).astype(F32)
    cache_v = cache_b_v.reshape(depth, ns, past_len * B_HEADS, HEAD_DIM).astype(F32)
    prompt_kv = None
    for l in range(depth):
        w = {
            "w_big": w_big, "w_gate4": w_gate4,
            "conv_w": conv_w[l].astype(F32),
            "a_log": _pad_vec(a_log[l]), "dt_bias": _pad_vec(dt_bias[l]),
            "a_norm_g": a_norm_g[l].reshape(1, HEAD_DIM).astype(F32),
            "f_bias": jnp.pad(f_bias[l].reshape(1, B_HEADS).astype(F32),
                              ((0, 0), (LANE_F_B, LANES - LANE_F_B - B_HEADS))),
            "c_w2p": jnp.pad(_pad_heads_c(c_w2[l]),
                             ((LANE_LR_C, LANES - LANE_LR_C - C_RANK), (0, 0))).astype(BF16),
            "c_b2p": _pad_heads_c(c_b2[l].reshape(1, C_KEY)).astype(F32),
            "c_norm_g": c_norm_g[l].reshape(1, C_DV).astype(F32),
            "w_out": w_out_b,
            "ln1_g": ln1_g[l], "ln1_b": ln1_b[l], "ln2_g": ln2_g[l], "ln2_b": ln2_b[l],
            "w_gate": w_gate.astype(F32), "w_up": w_up.astype(F32), "w_down": w_down_b,
        }
        hp, hpb, stp, prompt_kv = _layer(
            hp, hpb, nb, tp,
            jnp.zeros((nb, SUBLANES, A_CONV_DIM), F32),
            jnp.zeros((nb, A_HEADS, HEAD_DIM, HEAD_DIM), F32),
            None,
            jnp.zeros((nb, C_HEADS, C_DV, C_DK_PAD), F32), w, l, depth, alpha, prompt_kv)
        p_states.append(stp)
        buf8 = jnp.pad(state_a_conv[l].astype(F32), ((0, 0), (SUBLANES - (CONV_W - 1), 0), (0, 0)))
        s_c0t = jnp.pad(jnp.swapaxes(state_c_rec[l].astype(F32), 2, 3),
                        ((0, 0), (0, 0), (0, 0), (0, C_DK_PAD - C_DK)))
        hs, hsb, sts, _ = _layer(
            hs, hsb, ns, ts, buf8, state_a_rec[l].astype(F32),
            (cache_k, cache_v, cache_b_logf[l]), s_c0t, w, l, depth, alpha)
        s_states.append(sts)

    dp, ds = x_prompt.dtype, x_sample.dtype
    stack = lambda states, i, dt: jnp.stack([s[i] for s in states], axis=0).astype(dt)
    prompt_k, prompt_v = (a.reshape(depth, nb, tp, B_HEADS, HEAD_DIM).astype(dp) for a in prompt_kv)
    return ((hp.reshape(nb, tp, D_MODEL).astype(dp), hs.reshape(ns, ts, D_MODEL).astype(ds))
            + (stack(p_states, 0, dp), stack(p_states, 1, dp), prompt_k, prompt_v,
               stack(p_states, 4, dp), stack(p_states, 5, dp))
            + tuple(stack(s_states, i, ds) for i in range(6)))
```

```python
import functools

import jax
import jax.numpy as jnp
from jax import lax
from jax.experimental import pallas as pl
from jax.experimental.pallas import tpu as pltpu

F32 = jnp.float32
BF16 = jnp.bfloat16

D_MODEL = 4096
GDN_CHUNK = 64
GLA_CHUNK = 128
HEAD_DIM = 128
A_HEADS = 12
A_KEY = A_HEADS * HEAD_DIM
A_VAL = A_HEADS * HEAD_DIM
A_CONV_DIM = 2 * A_KEY + A_VAL
CONV_W = 4
B_HEADS = 8
B_WIDTH = B_HEADS * HEAD_DIM
C_HEADS = 4
C_DV = 384
C_DK = 192
C_DK_PAD = 256
C_KEY = C_HEADS * C_DK
C_KEY_PAD = C_HEADS * C_DK_PAD
C_VAL = C_HEADS * C_DV
C_RANK = 16
C_TAU = 16.0
D_FF = 11008
DEPTH_ALPHA_POW = 0.25
LN_EPS = 1e-5
RMS_EPS = 1e-6

LANES = 128
SUBLANES = 8
VMEM_LIMIT = 56 * 1024 * 1024

COL_QKV_A = 0
COL_Z_A = 4608
COL_V_C = 6144
COL_R_C = 7680
COL_Q_B = 9216
COL_K_B = 10240
COL_V_B = 11264
COL_Q_C = 12288
COL_K_C = 13312
IN_BIG = 14336
IN_GATE = LANES
LANE_A_A = 0
LANE_B_A = LANE_A_A + A_HEADS
LANE_F_B = LANE_B_A + A_HEADS
LANE_LR_C = LANE_F_B + B_HEADS

NEG_BIG = -1e30
LOG2E = 1.4426950408889634

PROJ_TM = 1024
PROJ_TN = 1024
FFN_TM = 2048
FFN_TN = 256
DOWN_TM = 512
DOWN_TN = 512
LN_TM = 512
FOX_TQ = 512
FOX_PREP_TB = 512
FOX_HEADS_PER_STEP = 4
FGATE_TB = 2048
FGATE_SUB = 256
GDN_SEQS_PER_STEP = 1
GDN_HEADS_PER_GROUP = 6
GDN_GROUP_SKEW = 1


def _cparams(sem):
    return pltpu.CompilerParams(dimension_semantics=sem, vmem_limit_bytes=VMEM_LIMIT)


def _sigmoid(x):
    return 1.0 / (1.0 + jnp.exp(-x))


def _silu(x):
    return x * _sigmoid(x)


def _softplus(x):
    return jnp.maximum(x, 0.0) + jnp.log(1.0 + jnp.exp(-jnp.abs(x)))


def _log_sigmoid(x):
    return -_softplus(-x)


def _split3(x):
    x1 = x.astype(BF16)
    r1 = x - x1.astype(F32)
    x2 = r1.astype(BF16)
    x3 = (r1 - x2.astype(F32)).astype(BF16)
    return x1, x2, x3


def _dot(a, b):
    return jnp.dot(a, b, preferred_element_type=F32)


def _dot_nt(a, b):
    return lax.dot_general(a, b, (((1,), (1,)), ((), ())), preferred_element_type=F32)


def _dot_tn(a, b):
    return lax.dot_general(a, b, (((0,), (0,)), ((), ())), preferred_element_type=F32)


def _dot_exact_lhs(a_bf16, x):
    x1, x2, x3 = _split3(x)
    return _dot(a_bf16, x1) + _dot(a_bf16, x2) + _dot(a_bf16, x3)


def _solve_dot(a, b):
    return _dot(a.astype(BF16), b.astype(BF16))


def _iota2(shape, dim):
    return lax.broadcasted_iota(jnp.int32, shape, dim)


def _tri_incl(n):
    return (_iota2((n, n), 1) <= _iota2((n, n), 0)).astype(BF16)


def _mm_kernel(x_ref, w_ref, o_ref):
    o_ref[...] = _dot(x_ref[...], w_ref[...]).astype(o_ref.dtype)


def _matmul(x, w, layer, tm, tn, out_dtype):
    m, k = x.shape
    n = w.shape[2]
    tm = min(tm, m)
    return pl.pallas_call(
        _mm_kernel,
        grid=(m // tm, n // tn),
        in_specs=[pl.BlockSpec((tm, k), lambda i, j: (i, 0)),
                  pl.BlockSpec((None, k, tn), lambda i, j: (layer, 0, j))],
        out_specs=pl.BlockSpec((tm, tn), lambda i, j: (i, j)),
        out_shape=jax.ShapeDtypeStruct((m, n), out_dtype),
        compiler_params=_cparams(("parallel", "parallel")),
    )(x, w)


def _lane_partial(v):
    part = v[:, 0:LANES]
    for c in range(1, v.shape[1] // LANES):
        part = part + v[:, c * LANES:(c + 1) * LANES]
    return part


def _residual_epilogue(alpha, r, acc, v_ref, s1_ref, s2_ref):
    v = alpha * r + acc
    v_ref[...] = v
    p1 = _lane_partial(v)
    p2 = _lane_partial(v * v)
    first = pl.program_id(1) == 0
    s1_ref[...] = jnp.where(first, p1, s1_ref[...] + p1)
    s2_ref[...] = jnp.where(first, p2, s2_ref[...] + p2)


def _row_stats(s1, s2, width):
    mu = jnp.sum(s1, axis=-1, keepdims=True) * (1.0 / width)
    var = jnp.sum(s2, axis=-1, keepdims=True) * (1.0 / width) - mu * mu
    return mu, lax.rsqrt(var + LN_EPS)


def _outproj_kernel(alpha, a_ref, b_ref, c_ref, w_ref, r_ref, v_ref, s1_ref, s2_ref):
    acc = (_dot(a_ref[...], w_ref[0:A_VAL, :])
           + _dot(b_ref[...], w_ref[A_VAL:A_VAL + B_WIDTH, :])
           + _dot(c_ref[...], w_ref[A_VAL + B_WIDTH:D_MODEL, :]))
    _residual_epilogue(alpha, r_ref[...], acc, v_ref, s1_ref, s2_ref)


def _resid_out(m, tm, tn):
    tile = pl.BlockSpec((tm, tn), lambda i, j: (i, j))
    stat = pl.BlockSpec((tm, LANES), lambda i, j: (i, 0))
    return ([tile, stat, stat],
            [jax.ShapeDtypeStruct((m, D_MODEL), F32)] + [jax.ShapeDtypeStruct((m, LANES), F32)] * 2)


def _outproj(oa, ob, oc, w, layer, resid, alpha, tm, tn):
    m = oa.shape[0]
    tm = min(tm, m)
    rows = lambda width: pl.BlockSpec((tm, width), lambda i, j: (i, 0))
    out_specs, out_shape = _resid_out(m, tm, tn)
    return pl.pallas_call(
        functools.partial(_outproj_kernel, alpha),
        grid=(m // tm, D_MODEL // tn),
        in_specs=[rows(A_VAL), rows(B_WIDTH), rows(C_VAL),
                  pl.BlockSpec((None, D_MODEL, tn), lambda i, j: (layer, 0, j)),
                  pl.BlockSpec((tm, tn), lambda i, j: (i, j))],
        out_specs=out_specs, out_shape=out_shape,
        compiler_params=_cparams(("parallel", "arbitrary")),
    )(oa, ob, oc, w, resid)


def _down_kernel(alpha, h_ref, w_ref, v1_ref, s1_ref, s2_ref, g_ref, b_ref, v_ref, t1_ref, t2_ref):
    acc = _dot(h_ref[...], w_ref[...])
    mu, rstd = _row_stats(s1_ref[...], s2_ref[...], D_MODEL)
    x1 = (v1_ref[...] - mu) * rstd * g_ref[...] + b_ref[...]
    _residual_epilogue(alpha, x1, acc, v_ref, t1_ref, t2_ref)


def _down_proj(hmid, w, layer, v1, s1, s2, g, b, alpha, tm, tn):
    m, k = hmid.shape
    tm = min(tm, m)
    tile = pl.BlockSpec((tm, tn), lambda i, j: (i, j))
    stat = pl.BlockSpec((tm, LANES), lambda i, j: (i, 0))
    vec = pl.BlockSpec((1, tn), lambda i, j: (0, j))
    out_specs, out_shape = _resid_out(m, tm, tn)
    return pl.pallas_call(
        functools.partial(_down_kernel, alpha),
        grid=(m // tm, D_MODEL // tn),
        in_specs=[pl.BlockSpec((tm, k), lambda i, j: (i, 0)),
                  pl.BlockSpec((None, k, tn), lambda i, j: (layer, 0, j)),
                  tile, stat, stat, vec, vec],
        out_specs=out_specs, out_shape=out_shape,
        compiler_params=_cparams(("parallel", "arbitrary")),
    )(hmid, w, v1, s1, s2, g.reshape(1, D_MODEL), b.reshape(1, D_MODEL))


def _swiglu_kernel(x_ref, wg_ref, wu_ref, o_ref):
    x = x_ref[...]
    a = _dot(x, wg_ref[...].astype(BF16))
    b = _dot(x, wu_ref[...].astype(BF16))
    o_ref[...] = (_silu(a) * b).astype(o_ref.dtype)


def _swiglu(x, wg, wu, layer, tm, tn):
    m, k = x.shape
    n = wg.shape[2]
    tm = min(tm, m)
    wspec = pl.BlockSpec((None, k, tn), lambda i, j: (layer, 0, j))
    return pl.pallas_call(
        _swiglu_kernel,
        grid=(m // tm, n // tn),
        in_specs=[pl.BlockSpec((tm, k), lambda i, j: (i, 0), pipeline_mode=pl.Buffered(1)), wspec, wspec],
        out_specs=pl.BlockSpec((tm, tn), lambda i, j: (i, j)),
        out_shape=jax.ShapeDtypeStruct((m, n), BF16),
        compiler_params=_cparams(("parallel", "parallel")),
    )(x, wg, wu)


def _ln_apply_kernel(with_f32, v_ref, s1_ref, s2_ref, g_ref, b_ref, *out_refs):
    mu, rstd = _row_stats(s1_ref[...], s2_ref[...], D_MODEL)
    out = (v_ref[...] - mu) * rstd * g_ref[...] + b_ref[...]
    out_refs[0][...] = out.astype(BF16)
    if with_f32:
        out_refs[1][...] = out


def _ln_apply(v, s1, s2, g, b, with_f32, tm=LN_TM):
    m, d = v.shape
    tm = min(tm, m)
    row = pl.BlockSpec((tm, d), lambda i: (i, 0))
    stat = pl.BlockSpec((tm, LANES), lambda i: (i, 0))
    vec = pl.BlockSpec((1, d), lambda i: (0, 0))
    n_out = 2 if with_f32 else 1
    return pl.pallas_call(
        functools.partial(_ln_apply_kernel, with_f32),
        grid=(m // tm,),
        in_specs=[row, stat, stat, vec, vec],
        out_specs=[row] * n_out,
        out_shape=[jax.ShapeDtypeStruct((m, d), BF16), jax.ShapeDtypeStruct((m, d), F32)][:n_out],
        compiler_params=_cparams(("parallel",)),
    )(v, s1, s2, g.reshape(1, d), b.reshape(1, d))


def _fgate_kernel(apply_gate, tb, x_ref, bias_ref, c0_ref, lf_ref, c_ref, carry):
    @pl.when(pl.program_id(1) == 0)
    def _():
        carry[...] = c0_ref[0]

    x = x_ref[0]
    lf = _log_sigmoid(x + bias_ref[...]) if apply_gate else x
    lf_ref[0] = lf
    sub = min(tb, FGATE_SUB)
    tri = _tri_incl(sub)
    parts = [_dot_exact_lhs(tri, lf[k * sub:(k + 1) * sub]) for k in range(tb // sub)]
    run = carry[...]
    for k, part in enumerate(parts):
        c_ref[0, k * sub:(k + 1) * sub, :] = part + run
        run = run + part[sub - 1:sub, :]
    carry[...] = run


def _fgate(x, bias, c0, apply_gate, tb):
    n, t, _ = x.shape
    tb = min(tb, t)
    blk = pl.BlockSpec((1, tb, LANES), lambda i, j: (i, j, 0))
    return pl.pallas_call(
        functools.partial(_fgate_kernel, apply_gate, tb),
        grid=(n, t // tb),
        in_specs=[pl.BlockSpec((1, tb, LANES), lambda i, j: (i, j, 0)),
                  pl.BlockSpec((1, LANES), lambda i, j: (0, 0)),
                  pl.BlockSpec((1, 1, LANES), lambda i, j: (i, 0, 0))],
        out_specs=[blk, blk],
        out_shape=[jax.ShapeDtypeStruct((n, t, LANES), F32)] * 2,
        scratch_shapes=[pltpu.VMEM((1, LANES), F32)],
        compiler_params=_cparams(("parallel", "arbitrary")),
    )(x, bias, c0)


def _gdn_kernel(L, NB, qkv_ref, z_ref, gate_ref, buf_ref, convw_ref, alog_ref, dtb_ref, ng_ref, s0_ref,
                o_ref, s_ref, xs):
    @pl.when(pl.program_id(1) == 0)
    def _():
        xs[:, 0:SUBLANES, :] = buf_ref[...]
        s_ref[...] = s0_ref[...]

    xs[:, SUBLANES:SUBLANES + L, :] = qkv_ref[...]

    def conv_cols(b, c0):
        r0 = SUBLANES - (CONV_W - 1)
        acc = xs[b, r0:r0 + L, c0:c0 + HEAD_DIM] * convw_ref[0:1, c0:c0 + HEAD_DIM]
        for i in range(1, CONV_W):
            acc = acc + xs[b, r0 + i:r0 + i + L, c0:c0 + HEAD_DIM] * convw_ref[i:i + 1, c0:c0 + HEAD_DIM]
        return _silu(acc)

    def l2n(x):
        return x * lax.rsqrt(jnp.sum(x * x, axis=-1, keepdims=True) + RMS_EPS)

    tri = _tri_incl(L)
    eye_l = (_iota2((LANES, LANES), 0) == _iota2((LANES, LANES), 1)).astype(BF16)
    seqs = []
    for b in range(NB):
        a_in = gate_ref[b]
        g = -jnp.exp(alog_ref[...]) * _softplus(a_in + dtb_ref[...])
        gc = _dot_exact_lhs(tri, g)
        g1, g2, g3 = _split3(gc)
        gct = _dot_nt(eye_l, g1) + _dot_nt(eye_l, g2) + _dot_nt(eye_l, g3)
        seqs.append(dict(gc=gc, gct=gct, beta=_sigmoid(a_in)))

    row = _iota2((L, L), 0)
    col = _iota2((L, L), 1)
    eye = (row == col).astype(F32)
    n_lvl = L.bit_length() - 2

    mm = _solve_dot

    def st_prep(hd):
        b, h = hd["b"], hd["h"]
        sq = seqs[b]
        q = l2n(conv_cols(b, h * HEAD_DIM)) * (HEAD_DIM ** -0.5)
        k = l2n(conv_cols(b, A_KEY + h * HEAD_DIM))
        v = conv_cols(b, 2 * A_KEY + h * HEAD_DIM)
        gcol = sq["gc"][:, h:h + 1]
        bcol = sq["beta"][:, LANE_B_A + h:LANE_B_A + h + 1]
        glast = sq["gc"][L - 1:L, h:h + 1]
        dec = jnp.exp(jnp.where(row >= col, gcol - sq["gct"][h:h + 1, :], NEG_BIG))
        eg = jnp.exp(gcol)
        qb = q.astype(BF16)
        kb = k.astype(BF16)
        p = -(bcol * _dot_nt(kb, kb) * jnp.where(row > col, dec, 0.0))
        hd.update(qb=qb, p=p, t=eye + p, eg=eg, e_last=jnp.exp(glast),
                  qkd=(_dot_nt(qb, kb) * dec).astype(BF16),
                  rhs=jnp.concatenate([(bcol * eg) * k, bcol * v], axis=1),
                  k_end=(k * jnp.exp(glast - gcol)).astype(BF16),
                  s=s_ref[b, h])

    def st_square(hd):
        hd["p"] = mm(hd["p"], hd["p"])

    def st_level(hd):
        pt = mm(hd["p"], jnp.concatenate([hd["p"], hd["t"]], axis=1))
        hd["t"] = hd["t"] + pt[:, L:2 * L]
        hd["p"] = pt[:, 0:L]

    def st_last_level(hd):
        hd["t"] = hd["t"] + mm(hd["p"], hd["t"])

    def st_solve(hd):
        hd["tr"] = mm(hd["t"], hd["rhs"])

    def st_state(hd):
        tr = hd["tr"]
        sb = hd["s"].astype(BF16)
        hd["ub"] = (tr[:, HEAD_DIM:2 * HEAD_DIM] - _dot(tr[:, 0:HEAD_DIM].astype(BF16), sb)).astype(BF16)
        hd["qs"] = _dot(hd["qb"], sb)

    def st_out(hd):
        b, h = hd["b"], hd["h"]
        o = hd["eg"] * hd["qs"] + _dot(hd["qkd"], hd["ub"])
        s_ref[b, h] = hd["e_last"] * hd["s"] + _dot_tn(hd["k_end"], hd["ub"])
        o = o * lax.rsqrt(jnp.mean(o * o, axis=-1, keepdims=True) + RMS_EPS) * ng_ref[...]
        zh = z_ref[b, :, h * HEAD_DIM:(h + 1) * HEAD_DIM]
        o_ref[b, :, h * HEAD_DIM:(h + 1) * HEAD_DIM] = (o * _silu(zh)).astype(BF16)
        hd.clear()

    stages = [st_prep, st_square] + [st_level] * (n_lvl - 1) + [st_last_level, st_solve, st_state, st_out]
    heads = [dict(b=b, h=h) for b in range(NB) for h in range(A_HEADS)]
    groups = [heads[i:i + GDN_HEADS_PER_GROUP] for i in range(0, len(heads), GDN_HEADS_PER_GROUP)]
    skew = GDN_GROUP_SKEW
    for wave in range(skew * (len(groups) - 1) + len(stages)):
        for g, group in enumerate(groups):
            if 0 <= wave - skew * g < len(stages):
                for hd in group:
                    stages[wave - skew * g](hd)

    tail = xs[:, L:L + SUBLANES, :]
    xs[:, 0:SUBLANES, :] = tail


def _gdn(hbig, hgate, conv_buf8, conv_w, a_log, dt_bias, norm_g, s0):
    n, t, _ = hbig.shape
    L = min(t, GDN_CHUNK)
    nb = GDN_SEQS_PER_STEP
    const2 = lambda i, j: (0, 0)
    state = pl.BlockSpec((nb, A_HEADS, HEAD_DIM, HEAD_DIM), lambda i, j: (i, 0, 0, 0))
    return pl.pallas_call(
        functools.partial(_gdn_kernel, L, nb),
        grid=(n // nb, t // L),
        in_specs=[pl.BlockSpec((nb, L, A_CONV_DIM), lambda i, j: (i, j, COL_QKV_A // A_CONV_DIM)),
                  pl.BlockSpec((nb, L, A_VAL), lambda i, j: (i, j, COL_Z_A // A_VAL)),
                  pl.BlockSpec((nb, L, LANES), lambda i, j: (i, j, 0)),
                  pl.BlockSpec((nb, SUBLANES, A_CONV_DIM), lambda i, j: (i, 0, 0)),
                  pl.BlockSpec((CONV_W, A_CONV_DIM), const2),
                  pl.BlockSpec((1, LANES), const2),
                  pl.BlockSpec((1, LANES), const2),
                  pl.BlockSpec((1, HEAD_DIM), const2),
                  state],
        out_specs=[pl.BlockSpec((nb, L, A_VAL), lambda i, j: (i, j, 0)), state],
        out_shape=[jax.ShapeDtypeStruct((n, t, A_VAL), BF16),
                   jax.ShapeDtypeStruct((n, A_HEADS, HEAD_DIM, HEAD_DIM), F32)],
        scratch_shapes=[pltpu.VMEM((nb, L + SUBLANES, A_CONV_DIM), F32)],
        compiler_params=_cparams(("parallel", "arbitrary")),
    )(hbig, hbig, hgate, conv_buf8, conv_w, a_log, dt_bias, norm_g, s0)


def _gla_kernel(L, q_ref, k_ref, v_ref, r_ref, gate_ref, w2_ref, b2_ref, ng_ref, s0_ref,
                o_ref, s_ref):
    @pl.when(pl.program_id(1) == 0)
    def _():
        s_ref[0] = s0_ref[0]

    z = _dot(gate_ref[0].astype(BF16), w2_ref[...]) + b2_ref[...]
    lg = _log_sigmoid(z) * (1.0 / C_TAU)
    b = _dot_exact_lhs(_tri_incl(L), lg)
    causal = _iota2((L, L), 0) >= _iota2((L, L), 1)

    heads = []
    for h in range(C_HEADS):
        ks = slice(h * C_DK_PAD, (h + 1) * C_DK_PAD)
        vs = slice(h * C_DV, (h + 1) * C_DV)
        bh = b[:, ks]
        blast = bh[L - 1:L, :]
        bref = bh[L // 2:L // 2 + 1, :]
        q = q_ref[0, :, ks] * (C_DK ** -0.5)
        k = k_ref[0, :, ks]
        heads.append(dict(
            vs=vs, vb=v_ref[0, :, vs].astype(BF16), st=s_ref[0, h],
            q_state=(q * jnp.exp(bh)).astype(BF16),
            q_in=(q * jnp.exp(bh - bref)).astype(BF16),
            k_in=(k * jnp.exp(bref - bh)).astype(BF16),
            k_end=(k * jnp.exp(blast - bh)).astype(BF16),
            e_last=jnp.exp(blast)))
    for hd in heads:
        hd["att"] = jnp.where(causal, _dot_nt(hd["q_in"], hd["k_in"]), 0.0).astype(BF16)
        hd["o"] = _dot_nt(hd["q_state"], hd["st"].astype(BF16))
        hd["st_new"] = hd["st"] * hd["e_last"] + _dot_tn(hd["vb"], hd["k_end"])
    for hd in heads:
        o = hd["o"] + _dot(hd["att"], hd["vb"])
        hd["o"] = o * lax.rsqrt(jnp.mean(o * o, axis=-1, keepdims=True) + RMS_EPS) * ng_ref[...]
    for h, hd in enumerate(heads):
        s_ref[0, h] = hd["st_new"]
        o_ref[0, :, hd["vs"]] = (hd["o"] * _silu(r_ref[0, :, hd["vs"]])).astype(BF16)


def _gla(hbig, hgate, w2p, b2p, norm_g, s0t):
    n, t, _ = hbig.shape
    L = min(t, GLA_CHUNK)
    const2 = lambda i, j: (0, 0)
    state = pl.BlockSpec((1, C_HEADS, C_DV, C_DK_PAD), lambda i, j: (i, 0, 0, 0))
    return pl.pallas_call(
        functools.partial(_gla_kernel, L),
        grid=(n, t // L),
        in_specs=[pl.BlockSpec((1, L, C_KEY_PAD), lambda i, j: (i, j, COL_Q_C // C_KEY_PAD)),
                  pl.BlockSpec((1, L, C_KEY_PAD), lambda i, j: (i, j, COL_K_C // C_KEY_PAD)),
                  pl.BlockSpec((1, L, C_VAL), lambda i, j: (i, j, COL_V_C // C_VAL)),
                  pl.BlockSpec((1, L, C_VAL), lambda i, j: (i, j, COL_R_C // C_VAL)),
                  pl.BlockSpec((1, L, LANES), lambda i, j: (i, j, 0)),
                  pl.BlockSpec((LANES, C_KEY_PAD), const2),
                  pl.BlockSpec((1, C_KEY_PAD), const2),
                  pl.BlockSpec((1, C_DV), const2),
                  state],
        out_specs=[pl.BlockSpec((1, L, C_VAL), lambda i, j: (i, j, 0)), state],
        out_shape=[jax.ShapeDtypeStruct((n, t, C_VAL), BF16),
                   jax.ShapeDtypeStruct((n, C_HEADS, C_DV, C_DK_PAD), F32)],
        compiler_params=_cparams(("parallel", "arbitrary")),
    )(hbig, hbig, hbig, hbig, hgate, w2p, b2p, norm_g, s0t)


def _fox_prep_kernel(q_ref, k_ref, v_ref, c_ref, qp_ref, kp_ref, vp_ref):
    tb = q_ref.shape[1]
    lane = _iota2((tb, LANES), 1)
    ones_q = jnp.where((lane >= 3) & (lane < 6), 1.0, 0.0)
    ones_k = jnp.where(lane < 3, 1.0, 0.0)
    c2 = c_ref[0] * LOG2E
    for h in range(B_HEADS):
        hs = slice(h * HEAD_DIM, (h + 1) * HEAD_DIM)
        c1, cm, cl = (t.astype(F32) for t in _split3(c2[:, LANE_F_B + h:LANE_F_B + h + 1]))
        ext_q = jnp.where(lane == 0, c1, jnp.where(lane == 1, cm, jnp.where(lane == 2, cl, ones_q)))
        ext_k = jnp.where(lane == 3, -c1, jnp.where(lane == 4, -cm, jnp.where(lane == 5, -cl, ones_k)))
        base = 2 * h * HEAD_DIM
        qp_ref[0, :, base:base + HEAD_DIM] = (q_ref[0, :, hs] * (HEAD_DIM ** -0.5 * LOG2E)).astype(BF16)
        qp_ref[0, :, base + HEAD_DIM:base + 2 * HEAD_DIM] = ext_q.astype(BF16)
        kp_ref[0, :, base:base + HEAD_DIM] = k_ref[0, :, hs].astype(BF16)
        kp_ref[0, :, base + HEAD_DIM:base + 2 * HEAD_DIM] = ext_k.astype(BF16)
    vp_ref[0] = v_ref[0].astype(BF16)


def _fox_prep_cache_kernel(q_ref, k_ref, v_ref, c_ref, *rest):
    qp_ref, kp_ref, vp_ref, kc_ref, vc_ref = rest[-5:]
    _fox_prep_kernel(q_ref, k_ref, v_ref, c_ref, qp_ref, kp_ref, vp_ref)
    kc_ref[0] = k_ref[0]
    vc_ref[0] = v_ref[0]


def _fox_prep(hbig, c, layer, depth, caches, tb=FOX_PREP_TB):
    n, t, _ = hbig.shape
    tb = min(tb, t)
    col = lambda blk: (lambda i, j: (i, j, blk))
    wide = pl.BlockSpec((1, tb, 2 * B_WIDTH), col(0))
    cache_spec = pl.BlockSpec((None, 1, tb, B_WIDTH), lambda i, j: (layer, i, j, 0))
    cache_shape = jax.ShapeDtypeStruct((depth, n, t, B_WIDTH), F32)
    in_specs = [pl.BlockSpec((1, tb, B_WIDTH), col(COL_Q_B // B_WIDTH)),
                pl.BlockSpec((1, tb, B_WIDTH), col(COL_K_B // B_WIDTH)),
                pl.BlockSpec((1, tb, B_WIDTH), col(COL_V_B // B_WIDTH)),
                pl.BlockSpec((1, tb, LANES), col(0))]
    args = [hbig, hbig, hbig, c]
    aliases = {}
    if caches is not None:
        in_specs += [pl.BlockSpec(memory_space=pl.ANY)] * 2
        args += list(caches)
        aliases = {4: 3, 5: 4}
    return pl.pallas_call(
        _fox_prep_cache_kernel,
        grid=(n, t // tb),
        in_specs=in_specs,
        out_specs=[wide, wide, pl.BlockSpec((1, tb, B_WIDTH), col(0)), cache_spec, cache_spec],
        out_shape=[jax.ShapeDtypeStruct((n, t, 2 * B_WIDTH), BF16),
                   jax.ShapeDtypeStruct((n, t, 2 * B_WIDTH), BF16),
                   jax.ShapeDtypeStruct((n, t, B_WIDTH), BF16),
                   cache_shape, cache_shape],
        input_output_aliases=aliases,
        compiler_params=_cparams(("parallel", "parallel")),
    )(*args)


def _fox_prompt_kernel(tq, q_ref, k_ref, v_ref, o_ref, m_sc, l_sc, acc_sc):
    qi = pl.program_id(2)
    m_sc[...] = jnp.full(m_sc.shape, NEG_BIG, F32)
    l_sc[...] = jnp.zeros(l_sc.shape, F32)
    acc_sc[...] = jnp.zeros(acc_sc.shape, F32)
    n_tiles = tq // LANES
    heads = range(FOX_HEADS_PER_STEP)
    qw = 2 * HEAD_DIM

    def block(j, masked):
        off = pl.multiple_of(j * tq, tq)
        ss = [_dot_nt(q_ref[0, :, h * qw:(h + 1) * qw], k_ref[0, pl.ds(off, tq), h * qw:(h + 1) * qw])
              for h in heads]
        if masked:
            visible = _iota2((tq, tq), 1) <= _iota2((tq, tq), 0)
            ss = [jnp.where(visible, s, NEG_BIG) for s in ss]
        new = []
        for h, s in zip(heads, ss):
            tiles = [s[:, c * LANES:(c + 1) * LANES] for c in range(n_tiles)]
            mx = tiles[0]
            for tl in tiles[1:]:
                mx = jnp.maximum(mx, tl)
            m_prev = m_sc[h]
            m_new = jnp.maximum(m_prev, jnp.max(mx, axis=-1, keepdims=True))
            alpha = jnp.exp2(m_prev - m_new)
            ps = [jnp.exp2(tl - m_new) for tl in tiles]
            psum = ps[0]
            for pt in ps[1:]:
                psum = psum + pt
            p = jnp.concatenate([pt.astype(BF16) for pt in ps], axis=1)
            pv = _dot(p, v_ref[0, pl.ds(off, tq), h * HEAD_DIM:(h + 1) * HEAD_DIM])
            new.append((m_new, alpha * l_sc[h] + psum, alpha * acc_sc[h] + pv))
        for h, (m_new, l_new, acc_new) in zip(heads, new):
            m_sc[h] = m_new
            l_sc[h] = l_new
            acc_sc[h] = acc_new

    def body(j, carry):
        block(j, False)
        return carry

    lax.fori_loop(0, qi, body, 0)
    block(qi, True)
    for h in heads:
        o_ref[0, :, h * HEAD_DIM:(h + 1) * HEAD_DIM] = (
            acc_sc[h] / jnp.sum(l_sc[h], axis=-1, keepdims=True)).astype(BF16)


def _fox_prompt(qp, kp, vp, tq=FOX_TQ):
    n, t, _ = vp.shape
    tq = min(tq, t)
    hps = FOX_HEADS_PER_STEP
    return pl.pallas_call(
        functools.partial(_fox_prompt_kernel, tq),
        grid=(n, B_HEADS // hps, t // tq),
        in_specs=[pl.BlockSpec((1, tq, hps * 2 * HEAD_DIM), lambda i, h, qi: (i, qi, h)),
                  pl.BlockSpec((1, t, hps * 2 * HEAD_DIM), lambda i, h, qi: (i, 0, h),
                               pipeline_mode=pl.Buffered(1)),
                  pl.BlockSpec((1, t, hps * HEAD_DIM), lambda i, h, qi: (i, 0, h),
                               pipeline_mode=pl.Buffered(1))],
        out_specs=pl.BlockSpec((1, tq, hps * HEAD_DIM), lambda i, h, qi: (i, qi, h)),
        out_shape=jax.ShapeDtypeStruct((n, t, B_WIDTH), BF16),
        scratch_shapes=[pltpu.VMEM((hps, tq, LANES), F32),
                        pltpu.VMEM((hps, tq, LANES), F32),
                        pltpu.VMEM((hps, tq, HEAD_DIM), F32)],
        compiler_params=_cparams(("parallel", "parallel", "arbitrary")),
    )(qp, kp, vp)


def _fox_sample_kernel(t, p, q_ref, kn_ref, vn_ref, kp_ref, vp_ref, cq_ref, ckp_ref, ckn_ref, o_ref):
    causal = _iota2((t, t), 1) <= _iota2((t, t), 0)
    for h in range(B_HEADS):
        hs = slice(h * HEAD_DIM, (h + 1) * HEAD_DIM)
        kp = kp_ref[0, pl.ds(h, p, stride=B_HEADS), :].astype(BF16)
        vp = vp_ref[0, pl.ds(h, p, stride=B_HEADS), :].astype(BF16)
        qb = (q_ref[0, :, hs] * (HEAD_DIM ** -0.5)).astype(BF16)
        cq = cq_ref[0, h]
        sp = _dot_nt(qb, kp) + (cq - ckp_ref[0, h])
        sn = _dot_nt(qb, kn_ref[0, :, hs].astype(BF16)) + (cq - ckn_ref[0, h])
        sn = jnp.where(causal, sn, NEG_BIG)
        m = jnp.maximum(jnp.max(sp, axis=-1, keepdims=True), jnp.max(sn, axis=-1, keepdims=True))
        pp = jnp.exp(sp - m)
        pn = jnp.exp(sn - m)
        den = jnp.sum(pp, axis=-1, keepdims=True) + jnp.sum(pn, axis=-1, keepdims=True)
        o = _dot(pp.astype(BF16), vp) + _dot(pn.astype(BF16), vn_ref[0, :, hs].astype(BF16))
        o_ref[0, :, hs] = (o / den).astype(BF16)


def _fox_sample(hbig, k_past, v_past, layer, cq, ckp, ckn):
    n, t, _ = hbig.shape
    rows = k_past.shape[2]
    p = rows // B_HEADS
    col = lambda base: (lambda i: (i, 0, base // B_WIDTH))
    past = pl.BlockSpec((None, 1, rows, HEAD_DIM), lambda i: (layer, i, 0, 0))
    whole = lambda a: pl.BlockSpec((1,) + a.shape[1:], lambda i: (i, 0, 0, 0))
    return pl.pallas_call(
        functools.partial(_fox_sample_kernel, t, p),
        grid=(n,),
        in_specs=[pl.BlockSpec((1, t, B_WIDTH), col(COL_Q_B)),
                  pl.BlockSpec((1, t, B_WIDTH), col(COL_K_B)),
                  pl.BlockSpec((1, t, B_WIDTH), col(COL_V_B)),
                  past, past, whole(cq), whole(ckp), whole(ckn)],
        out_specs=pl.BlockSpec((1, t, B_WIDTH), lambda i: (i, 0, 0)),
        out_shape=jax.ShapeDtypeStruct((n, t, B_WIDTH), BF16),
        compiler_params=_cparams(("parallel",)),
    )(hbig, hbig, hbig, k_past, v_past, cq, ckp, ckn)


def _pad_cols(w, width):
    return jnp.pad(w, [(0, 0)] * (w.ndim - 1) + [(0, width - w.shape[-1])])


def _pad_heads_c(w):
    lead = w.shape[:-1]
    w = w.reshape(lead + (C_HEADS, C_DK))
    w = jnp.pad(w, [(0, 0)] * len(lead) + [(0, 0), (0, C_DK_PAD - C_DK)])
    return w.reshape(lead + (C_KEY_PAD,))


def _prep_w_in(w):
    o = 0
    parts = {}
    for name, size in (("qkv_a", A_CONV_DIM), ("z_a", A_VAL), ("a_a", A_HEADS), ("b_a", A_HEADS),
                       ("q_b", B_WIDTH), ("k_b", B_WIDTH), ("v_b", B_WIDTH), ("f_b", B_HEADS),
                       ("q_c", C_KEY), ("k_c", C_KEY), ("v_c", C_VAL), ("r_c", C_VAL), ("lr_c", C_RANK)):
        parts[name] = w[..., o:o + size].astype(BF16)
        o += size
    big = jnp.concatenate([parts["qkv_a"], parts["z_a"], parts["v_c"], parts["r_c"],
                           parts["q_b"], parts["k_b"], parts["v_b"],
                           _pad_heads_c(parts["q_c"]), _pad_heads_c(parts["k_c"])], axis=-1)
    gate = _pad_cols(jnp.concatenate([parts[nm] for nm in ("a_a", "b_a", "f_b", "lr_c")], axis=-1), LANES)
    return big, gate


def _pad_vec(v):
    return _pad_cols(v.reshape(1, -1).astype(F32), LANES)


def _mixer(x, xb, n, t, conv_buf8, s_a0, past_b, s_c0t, w, layer, depth, alpha, kv_caches):
    m = n * t
    hbig = _matmul(xb, w["w_big"], layer, PROJ_TM, PROJ_TN, F32).reshape(n, t, IN_BIG)
    hgate = _matmul(xb, w["w_gate4"], layer, PROJ_TM, IN_GATE, F32).reshape(n, t, IN_GATE)

    oa, s_a = _gdn(hbig, hgate, conv_buf8, w["conv_w"], w["a_log"], w["dt_bias"], w["a_norm_g"], s_a0)
    oc, s_ct = _gla(hbig, hgate, w["c_w2p"], w["c_b2p"], w["c_norm_g"], s_c0t)

    zero_c = jnp.zeros((n, 1, LANES), F32)
    if past_b is None:
        lf, c = _fgate(hgate, w["f_bias"], zero_c, True, FGATE_TB)
        qp, kp, vp, k_cache, v_cache = _fox_prep(hbig, c, layer, depth, kv_caches)
        kv_caches = (k_cache, v_cache)
        ob = _fox_prompt(qp, kp, vp)
        kb = vb = None
    else:
        k_past, v_past, lf_past = past_b
        p = lf_past.shape[1]
        lf_past = jnp.pad(lf_past.astype(F32), ((0, 0), (0, 0), (LANE_F_B, LANES - LANE_F_B - B_HEADS)))
        _, c_past = _fgate(lf_past, w["f_bias"], zero_c, False, FGATE_TB)
        lf, c = _fgate(hgate, w["f_bias"], c_past[:, p - 1:p, :], True, t)
        fb = slice(LANE_F_B, LANE_F_B + B_HEADS)
        cq = jnp.swapaxes(c[:, :, fb], 1, 2)[..., None]
        ckn = jnp.swapaxes(c[:, :, fb], 1, 2)[:, :, None, :]
        ckp = jnp.swapaxes(c_past[:, :, fb], 1, 2)[:, :, None, :]
        ob = _fox_sample(hbig, k_past, v_past, layer, cq, ckp, ckn)

    y = _outproj(oa.reshape(m, A_VAL), ob.reshape(m, B_WIDTH), oc.reshape(m, C_VAL),
                 w["w_out"], layer, x, alpha, PROJ_TM, PROJ_TN)

    conv_new = hbig[:, t - (CONV_W - 1):, COL_QKV_A:COL_QKV_A + A_CONV_DIM]
    if past_b is not None:
        kb = hbig[:, :, COL_K_B:COL_K_B + B_WIDTH].reshape(n, t, B_HEADS, HEAD_DIM)
        vb = hbig[:, :, COL_V_B:COL_V_B + B_WIDTH].reshape(n, t, B_HEADS, HEAD_DIM)
    s_c = jnp.swapaxes(s_ct, 2, 3)[:, :, :C_DK, :]
    return y, (conv_new, s_a, kb, vb, lf[:, :, LANE_F_B:LANE_F_B + B_HEADS], s_c), kv_caches


def _layer(x, xb, n, t, conv_buf8, s_a0, past_b, s_c0t, w, layer, depth, alpha, kv_caches=None):
    (v1, s1, s2), st, kv_caches = _mixer(x, xb, n, t, conv_buf8, s_a0, past_b, s_c0t, w, layer, depth,
                                         alpha, kv_caches)
    x1b, = _ln_apply(v1, s1, s2, w["ln1_g"], w["ln1_b"], False)
    hmid = _swiglu(x1b, w["w_gate"], w["w_up"], layer, FFN_TM, FFN_TN)
    v2, t1, t2 = _down_proj(hmid, w["w_down"], layer, v1, s1, s2, w["ln1_g"], w["ln1_b"], alpha,
                            DOWN_TM, DOWN_TN)
    xb, x = _ln_apply(v2, t1, t2, w["ln2_g"], w["ln2_b"], True)
    return x, xb, st, kv_caches


def kernel(x_prompt, x_sample, state_a_conv, state_a_rec, cache_b_k, cache_b_v, cache_b_logf, state_c_rec, w_in, conv_w, a_log, dt_bias, a_norm_g, f_bias, c_w2, c_b2, c_norm_g, w_out, ln1_g, ln1_b, w_gate, w_up, w_down, ln2_g, ln2_b):
    depth = w_in.shape[0]
    alpha = (2 * depth) ** DEPTH_ALPHA_POW
    nb, tp, _ = x_prompt.shape
    ns, ts, _ = x_sample.shape

    hp = x_prompt.reshape(nb * tp, D_MODEL).astype(F32)
    hs = x_sample.reshape(ns * ts, D_MODEL).astype(F32)
    hpb = hp.astype(BF16)
    hsb = hs.astype(BF16)
    p_states, s_states = [], []
    w_big, w_gate4 = _prep_w_in(w_in)
    w_out_b, w_down_b = w_out.astype(BF16), w_down.astype(BF16)
    past_len = cache_b_k.shape[2]
    cache_k = cache_b_k.reshape(depth, ns, past_len * B_HEADS, HEAD_DIM).astype(F32)
    cache_v = cache_b_v.reshape(depth, ns, past_len * B_HEADS, HEAD_DIM).astype(F32)
    prompt_kv = None
    for l in range(depth):
        w = {
            "w_big": w_big, "w_gate4": w_gate4,
            "conv_w": conv_w[l].astype(F32),
            "a_log": _pad_vec(a_log[l]), "dt_bias": _pad_vec(dt_bias[l]),
            "a_norm_g": a_norm_g[l].reshape(1, HEAD_DIM).astype(F32),
            "f_bias": jnp.pad(f_bias[l].reshape(1, B_HEADS).astype(F32),
                              ((0, 0), (LANE_F_B, LANES - LANE_F_B - B_HEADS))),
            "c_w2p": jnp.pad(_pad_heads_c(c_w2[l]),
                             ((LANE_LR_C, LANES - LANE_LR_C - C_RANK), (0, 0))).astype(BF16),
            "c_b2p": _pad_heads_c(c_b2[l].reshape(1, C_KEY)).astype(F32),
            "c_norm_g": c_norm_g[l].reshape(1, C_DV).astype(F32),
            "w_out": w_out_b,
            "ln1_g": ln1_g[l], "ln1_b": ln1_b[l], "ln2_g": ln2_g[l], "ln2_b": ln2_b[l],
            "w_gate": w_gate.astype(F32), "w_up": w_up.astype(F32), "w_down": w_down_b,
        }
        hp, hpb, stp, prompt_kv = _layer(
            hp, hpb, nb, tp,
            jnp.zeros((nb, SUBLANES, A_CONV_DIM), F32),
            jnp.zeros((nb, A_HEADS, HEAD_DIM, HEAD_DIM), F32),
            None,
            jnp.zeros((nb, C_HEADS, C_DV, C_DK_PAD), F32), w, l, depth, alpha, prompt_kv)
        p_states.append(stp)
        buf8 = jnp.pad(state_a_conv[l].astype(F32), ((0, 0), (SUBLANES - (CONV_W - 1), 0), (0, 0)))
        s_c0t = jnp.pad(jnp.swapaxes(state_c_rec[l].astype(F32), 2, 3),
                        ((0, 0), (0, 0), (0, 0), (0, C_DK_PAD - C_DK)))
        hs, hsb, sts, _ = _layer(
            hs, hsb, ns, ts, buf8, state_a_rec[l].astype(F32),
            (cache_k, cache_v, cache_b_logf[l]), s_c0t, w, l, depth, alpha)
        s_states.append(sts)

    dp, ds = x_prompt.dtype, x_sample.dtype
    stack = lambda states, i, dt: jnp.stack([s[i] for s in states], axis=0).astype(dt)
    prompt_k, prompt_v = (a.reshape(depth, nb, tp, B_HEADS, HEAD_DIM).astype(dp) for a in prompt_kv)
    return ((hp.reshape(nb, tp, D_MODEL).astype(dp), hs.reshape(ns, ts, D_MODEL).astype(ds))
            + (stack(p_states, 0, dp), stack(p_states, 1, dp), prompt_k, prompt_v,
               stack(p_states, 4, dp), stack(p_states, 5, dp))
            + tuple(stack(s_states, i, ds) for i in range(6)))
```

```python
import functools

import jax
import jax.numpy as jnp
from jax import lax
from jax.experimental import pallas as pl
from jax.experimental.pallas import tpu as pltpu

F32 = jnp.float32
BF16 = jnp.bfloat16

D_MODEL = 4096
GDN_CHUNK = 64
GLA_CHUNK = 128
HEAD_DIM = 128
A_HEADS = 12
A_KEY = A_HEADS * HEAD_DIM
A_VAL = A_HEADS * HEAD_DIM
A_CONV_DIM = 2 * A_KEY + A_VAL
CONV_W = 4
B_HEADS = 8
B_WIDTH = B_HEADS * HEAD_DIM
C_HEADS = 4
C_DV = 384
C_DK = 192
C_DK_PAD = 256
C_KEY = C_HEADS * C_DK
C_KEY_PAD = C_HEADS * C_DK_PAD
C_VAL = C_HEADS * C_DV
C_RANK = 16
C_TAU = 16.0
D_FF = 11008
DEPTH_ALPHA_POW = 0.25
LN_EPS = 1e-5
RMS_EPS = 1e-6

LANES = 128
SUBLANES = 8
VMEM_LIMIT = 56 * 1024 * 1024

COL_QKV_A = 0
COL_Z_A = 4608
COL_V_C = 6144
COL_R_C = 7680
COL_Q_B = 9216
COL_K_B = 10240
COL_V_B = 11264
COL_Q_C = 12288
COL_K_C = 13312
IN_BIG = 14336
IN_GATE = LANES
LANE_A_A = 0
LANE_B_A = LANE_A_A + A_HEADS
LANE_F_B = LANE_B_A + A_HEADS
LANE_LR_C = LANE_F_B + B_HEADS

NEG_BIG = -1e30
LOG2E = 1.4426950408889634

PROJ_TM = 1024
PROJ_TN = 1024
FFN_TM = 2048
FFN_TN = 256
DOWN_TM = 512
DOWN_TN = 512
LN_TM = 512
FOX_TQ = 512
FOX_PREP_TB = 512
FOX_HEADS_PER_STEP = 4
FGATE_TB = 2048
FGATE_SUB = 256
GDN_SEQS_PER_STEP = 1
GDN_HEADS_PER_GROUP = 6
GDN_GROUP_SKEW = 1


def _cparams(sem):
    return pltpu.CompilerParams(dimension_semantics=sem, vmem_limit_bytes=VMEM_LIMIT)


def _sigmoid(x):
    return 1.0 / (1.0 + jnp.exp(-x))


def _silu(x):
    return x * _sigmoid(x)


def _softplus(x):
    return jnp.maximum(x, 0.0) + jnp.log(1.0 + jnp.exp(-jnp.abs(x)))


def _log_sigmoid(x):
    return -_softplus(-x)


def _split3(x):
    x1 = x.astype(BF16)
    r1 = x - x1.astype(F32)
    x2 = r1.astype(BF16)
    x3 = (r1 - x2.astype(F32)).astype(BF16)
    return x1, x2, x3


def _dot(a, b):
    return jnp.dot(a, b, preferred_element_type=F32)


def _dot_nt(a, b):
    return lax.dot_general(a, b, (((1,), (1,)), ((), ())), preferred_element_type=F32)


def _dot_tn(a, b):
    return lax.dot_general(a, b, (((0,), (0,)), ((), ())), preferred_element_type=F32)


def _dot_exact_lhs(a_bf16, x):
    x1, x2, x3 = _split3(x)
    return _dot(a_bf16, x1) + _dot(a_bf16, x2) + _dot(a_bf16, x3)


def _solve_dot(a, b):
    return _dot(a.astype(BF16), b.astype(BF16))


def _iota2(shape, dim):
    return lax.broadcasted_iota(jnp.int32, shape, dim)


def _tri_incl(n):
    return (_iota2((n, n), 1) <= _iota2((n, n), 0)).astype(BF16)


def _mm_kernel(x_ref, w_ref, o_ref):
    o_ref[...] = _dot(x_ref[...], w_ref[...]).astype(o_ref.dtype)


def _matmul(x, w, layer, tm, tn, out_dtype):
    m, k = x.shape
    n = w.shape[2]
    tm = min(tm, m)
    return pl.pallas_call(
        _mm_kernel,
        grid=(m // tm, n // tn),
        in_specs=[pl.BlockSpec((tm, k), lambda i, j: (i, 0)),
                  pl.BlockSpec((None, k, tn), lambda i, j: (layer, 0, j))],
        out_specs=pl.BlockSpec((tm, tn), lambda i, j: (i, j)),
        out_shape=jax.ShapeDtypeStruct((m, n), out_dtype),
        compiler_params=_cparams(("parallel", "parallel")),
    )(x, w)


def _lane_partial(v):
    part = v[:, 0:LANES]
    for c in range(1, v.shape[1] // LANES):
        part = part + v[:, c * LANES:(c + 1) * LANES]
    return part


def _residual_epilogue(alpha, r, acc, v_ref, s1_ref, s2_ref):
    v = alpha * r + acc
    v_ref[...] = v
    p1 = _lane_partial(v)
    p2 = _lane_partial(v * v)
    first = pl.program_id(1) == 0
    s1_ref[...] = jnp.where(first, p1, s1_ref[...] + p1)
    s2_ref[...] = jnp.where(first, p2, s2_ref[...] + p2)


def _row_stats(s1, s2, width):
    mu = jnp.sum(s1, axis=-1, keepdims=True) * (1.0 / width)
    var = jnp.sum(s2, axis=-1, keepdims=True) * (1.0 / width) - mu * mu
    return mu, lax.rsqrt(var + LN_EPS)


def _outproj_kernel(alpha, a_ref, b_ref, c_ref, w_ref, r_ref, v_ref, s1_ref, s2_ref):
    acc = (_dot(a_ref[...], w_ref[0:A_VAL, :])
           + _dot(b_ref[...], w_ref[A_VAL:A_VAL + B_WIDTH, :])
           + _dot(c_ref[...], w_ref[A_VAL + B_WIDTH:D_MODEL, :]))
    _residual_epilogue(alpha, r_ref[...], acc, v_ref, s1_ref, s2_ref)


def _resid_out(m, tm, tn):
    tile = pl.BlockSpec((tm, tn), lambda i, j: (i, j))
    stat = pl.BlockSpec((tm, LANES), lambda i, j: (i, 0))
    return ([tile, stat, stat],
            [jax.ShapeDtypeStruct((m, D_MODEL), F32)] + [jax.ShapeDtypeStruct((m, LANES), F32)] * 2)


def _outproj(oa, ob, oc, w, layer, resid, alpha, tm, tn):
    m = oa.shape[0]
    tm = min(tm, m)
    rows = lambda width: pl.BlockSpec((tm, width), lambda i, j: (i, 0))
    out_specs, out_shape = _resid_out(m, tm, tn)
    return pl.pallas_call(
        functools.partial(_outproj_kernel, alpha),
        grid=(m // tm, D_MODEL // tn),
        in_specs=[rows(A_VAL), rows(B_WIDTH), rows(C_VAL),
                  pl.BlockSpec((None, D_MODEL, tn), lambda i, j: (layer, 0, j)),
                  pl.BlockSpec((tm, tn), lambda i, j: (i, j))],
        out_specs=out_specs, out_shape=out_shape,
        compiler_params=_cparams(("parallel", "arbitrary")),
    )(oa, ob, oc, w, resid)


def _down_kernel(alpha, h_ref, w_ref, v1_ref, s1_ref, s2_ref, g_ref, b_ref, v_ref, t1_ref, t2_ref):
    acc = _dot(h_ref[...], w_ref[...])
    mu, rstd = _row_stats(s1_ref[...], s2_ref[...], D_MODEL)
    x1 = (v1_ref[...] - mu) * rstd * g_ref[...] + b_ref[...]
    _residual_epilogue(alpha, x1, acc, v_ref, t1_ref, t2_ref)


def _down_proj(hmid, w, layer, v1, s1, s2, g, b, alpha, tm, tn):
    m, k = hmid.shape
    tm = min(tm, m)
    tile = pl.BlockSpec((tm, tn), lambda i, j: (i, j))
    stat = pl.BlockSpec((tm, LANES), lambda i, j: (i, 0))
    vec = pl.BlockSpec((1, tn), lambda i, j: (0, j))
    out_specs, out_shape = _resid_out(m, tm, tn)
    return pl.pallas_call(
        functools.partial(_down_kernel, alpha),
        grid=(m // tm, D_MODEL // tn),
        in_specs=[pl.BlockSpec((tm, k), lambda i, j: (i, 0)),
                  pl.BlockSpec((None, k, tn), lambda i, j: (layer, 0, j)),
                  tile, stat, stat, vec, vec],
        out_specs=out_specs, out_shape=out_shape,
        compiler_params=_cparams(("parallel", "arbitrary")),
    )(hmid, w, v1, s1, s2, g.reshape(1, D_MODEL), b.reshape(1, D_MODEL))


def _swiglu_kernel(x_ref, wg_ref, wu_ref, o_ref):
    x = x_ref[...]
    a = _dot(x, wg_ref[...].astype(BF16))
    b = _dot(x, wu_ref[...].astype(BF16))
    o_ref[...] = (_silu(a) * b).astype(o_ref.dtype)


def _swiglu(x, wg, wu, layer, tm, tn):
    m, k = x.shape
    n = wg.shape[2]
    tm = min(tm, m)
    wspec = pl.BlockSpec((None, k, tn), lambda i, j: (layer, 0, j))
    return pl.pallas_call(
        _swiglu_kernel,
        grid=(m // tm, n // tn),
        in_specs=[pl.BlockSpec((tm, k), lambda i, j: (i, 0), pipeline_mode=pl.Buffered(1)), wspec, wspec],
        out_specs=pl.BlockSpec((tm, tn), lambda i, j: (i, j)),
        out_shape=jax.ShapeDtypeStruct((m, n), BF16),
        compiler_params=_cparams(("parallel", "parallel")),
    )(x, wg, wu)


def _ln_apply_kernel(v_ref, s1_ref, s2_ref, g_ref, b_ref, *out_refs):
    mu, rstd = _row_stats(s1_ref[...], s2_ref[...], D_MODEL)
    out = (v_ref[...] - mu) * rstd * g_ref[...] + b_ref[...]
    for o_ref in out_refs:
        o_ref[...] = out.astype(o_ref.dtype)


def _ln_apply(v, s1, s2, g, b, dtypes, tm=LN_TM):
    m, d = v.shape
    tm = min(tm, m)
    row = pl.BlockSpec((tm, d), lambda i: (i, 0))
    stat = pl.BlockSpec((tm, LANES), lambda i: (i, 0))
    vec = pl.BlockSpec((1, d), lambda i: (0, 0))
    return pl.pallas_call(
        _ln_apply_kernel,
        grid=(m // tm,),
        in_specs=[row, stat, stat, vec, vec],
        out_specs=[row] * len(dtypes),
        out_shape=[jax.ShapeDtypeStruct((m, d), dt) for dt in dtypes],
        compiler_params=_cparams(("parallel",)),
    )(v, s1, s2, g.reshape(1, d), b.reshape(1, d))


def _fgate_kernel(apply_gate, tb, x_ref, bias_ref, c0_ref, lf_ref, c_ref, carry):
    @pl.when(pl.program_id(1) == 0)
    def _():
        carry[...] = c0_ref[0]

    x = x_ref[0]
    lf = _log_sigmoid(x + bias_ref[...]) if apply_gate else x
    lf_ref[0] = lf
    sub = min(tb, FGATE_SUB)
    tri = _tri_incl(sub)
    parts = [_dot_exact_lhs(tri, lf[k * sub:(k + 1) * sub]) for k in range(tb // sub)]
    run = carry[...]
    for k, part in enumerate(parts):
        c_ref[0, k * sub:(k + 1) * sub, :] = part + run
        run = run + part[sub - 1:sub, :]
    carry[...] = run


def _fgate(x, bias, c0, apply_gate, tb):
    n, t, _ = x.shape
    tb = min(tb, t)
    blk = pl.BlockSpec((1, tb, LANES), lambda i, j: (i, j, 0))
    return pl.pallas_call(
        functools.partial(_fgate_kernel, apply_gate, tb),
        grid=(n, t // tb),
        in_specs=[pl.BlockSpec((1, tb, LANES), lambda i, j: (i, j, 0)),
                  pl.BlockSpec((1, LANES), lambda i, j: (0, 0)),
                  pl.BlockSpec((1, 1, LANES), lambda i, j: (i, 0, 0))],
        out_specs=[blk, blk],
        out_shape=[jax.ShapeDtypeStruct((n, t, LANES), F32)] * 2,
        scratch_shapes=[pltpu.VMEM((1, LANES), F32)],
        compiler_params=_cparams(("parallel", "arbitrary")),
    )(x, bias, c0)


def _gdn_kernel(L, NB, qkv_ref, z_ref, gate_ref, buf_ref, convw_ref, alog_ref, dtb_ref, ng_ref, s0_ref,
                o_ref, s_ref, xs):
    @pl.when(pl.program_id(1) == 0)
    def _():
        xs[:, 0:SUBLANES, :] = buf_ref[...]
        s_ref[...] = s0_ref[...]

    xs[:, SUBLANES:SUBLANES + L, :] = qkv_ref[...]

    def conv_cols(b, c0):
        r0 = SUBLANES - (CONV_W - 1)
        acc = xs[b, r0:r0 + L, c0:c0 + HEAD_DIM] * convw_ref[0:1, c0:c0 + HEAD_DIM]
        for i in range(1, CONV_W):
            acc = acc + xs[b, r0 + i:r0 + i + L, c0:c0 + HEAD_DIM] * convw_ref[i:i + 1, c0:c0 + HEAD_DIM]
        return _silu(acc)

    def l2n(x):
        return x * lax.rsqrt(jnp.sum(x * x, axis=-1, keepdims=True) + RMS_EPS)

    tri = _tri_incl(L)
    eye_l = (_iota2((LANES, LANES), 0) == _iota2((LANES, LANES), 1)).astype(BF16)
    seqs = []
    for b in range(NB):
        a_in = gate_ref[b]
        g = -jnp.exp(alog_ref[...]) * _softplus(a_in + dtb_ref[...])
        gc = _dot_exact_lhs(tri, g)
        g1, g2, g3 = _split3(gc)
        gct = _dot_nt(eye_l, g1) + _dot_nt(eye_l, g2) + _dot_nt(eye_l, g3)
        seqs.append(dict(gc=gc, gct=gct, beta=_sigmoid(a_in)))

    row = _iota2((L, L), 0)
    col = _iota2((L, L), 1)
    eye = (row == col).astype(F32)
    n_lvl = L.bit_length() - 2

    mm = _solve_dot

    def st_prep(hd):
        b, h = hd["b"], hd["h"]
        sq = seqs[b]
        q = l2n(conv_cols(b, h * HEAD_DIM)) * (HEAD_DIM ** -0.5)
        k = l2n(conv_cols(b, A_KEY + h * HEAD_DIM))
        v = conv_cols(b, 2 * A_KEY + h * HEAD_DIM)
        gcol = sq["gc"][:, h:h + 1]
        bcol = sq["beta"][:, LANE_B_A + h:LANE_B_A + h + 1]
        glast = sq["gc"][L - 1:L, h:h + 1]
        dec = jnp.exp(jnp.where(row >= col, gcol - sq["gct"][h:h + 1, :], NEG_BIG))
        eg = jnp.exp(gcol)
        qb = q.astype(BF16)
        kb = k.astype(BF16)
        p = -(bcol * _dot_nt(kb, kb) * jnp.where(row > col, dec, 0.0))
        hd.update(qb=qb, p=p, t=eye + p, eg=eg, e_last=jnp.exp(glast),
                  qkd=(_dot_nt(qb, kb) * dec).astype(BF16),
                  rhs=jnp.concatenate([(bcol * eg) * k, bcol * v], axis=1),
                  k_end=(k * jnp.exp(glast - gcol)).astype(BF16),
                  s=s_ref[b, h])

    def st_square(hd):
        hd["p"] = mm(hd["p"], hd["p"])

    def st_level(hd):
        pt = mm(hd["p"], jnp.concatenate([hd["p"], hd["t"]], axis=1))
        hd["t"] = hd["t"] + pt[:, L:2 * L]
        hd["p"] = pt[:, 0:L]

    def st_last_level(hd):
        hd["t"] = hd["t"] + mm(hd["p"], hd["t"])

    def st_solve(hd):
        hd["tr"] = mm(hd["t"], hd["rhs"])

    def st_state(hd):
        tr = hd["tr"]
        sb = hd["s"].astype(BF16)
        hd["ub"] = (tr[:, HEAD_DIM:2 * HEAD_DIM] - _dot(tr[:, 0:HEAD_DIM].astype(BF16), sb)).astype(BF16)
        hd["qs"] = _dot(hd["qb"], sb)

    def st_out(hd):
        b, h = hd["b"], hd["h"]
        o = hd["eg"] * hd["qs"] + _dot(hd["qkd"], hd["ub"])
        s_ref[b, h] = hd["e_last"] * hd["s"] + _dot_tn(hd["k_end"], hd["ub"])
        o = o * lax.rsqrt(jnp.mean(o * o, axis=-1, keepdims=True) + RMS_EPS) * ng_ref[...]
        zh = z_ref[b, :, h * HEAD_DIM:(h + 1) * HEAD_DIM]
        o_ref[b, :, h * HEAD_DIM:(h + 1) * HEAD_DIM] = (o * _silu(zh)).astype(BF16)
        hd.clear()

    stages = [st_prep, st_square] + [st_level] * (n_lvl - 1) + [st_last_level, st_solve, st_state, st_out]
    heads = [dict(b=b, h=h) for b in range(NB) for h in range(A_HEADS)]
    groups = [heads[i:i + GDN_HEADS_PER_GROUP] for i in range(0, len(heads), GDN_HEADS_PER_GROUP)]
    skew = GDN_GROUP_SKEW
    for wave in range(skew * (len(groups) - 1) + len(stages)):
        for g, group in enumerate(groups):
            if 0 <= wave - skew * g < len(stages):
                for hd in group:
                    stages[wave - skew * g](hd)

    tail = xs[:, L:L + SUBLANES, :]
    xs[:, 0:SUBLANES, :] = tail


def _gdn(hbig, hgate, conv_buf8, conv_w, a_log, dt_bias, norm_g, s0):
    n, t, _ = hbig.shape
    L = min(t, GDN_CHUNK)
    nb = GDN_SEQS_PER_STEP
    const2 = lambda i, j: (0, 0)
    state = pl.BlockSpec((nb, A_HEADS, HEAD_DIM, HEAD_DIM), lambda i, j: (i, 0, 0, 0))
    return pl.pallas_call(
        functools.partial(_gdn_kernel, L, nb),
        grid=(n // nb, t // L),
        in_specs=[pl.BlockSpec((nb, L, A_CONV_DIM), lambda i, j: (i, j, COL_QKV_A // A_CONV_DIM)),
                  pl.BlockSpec((nb, L, A_VAL), lambda i, j: (i, j, COL_Z_A // A_VAL)),
                  pl.BlockSpec((nb, L, LANES), lambda i, j: (i, j, 0)),
                  pl.BlockSpec((nb, SUBLANES, A_CONV_DIM), lambda i, j: (i, 0, 0)),
                  pl.BlockSpec((CONV_W, A_CONV_DIM), const2),
                  pl.BlockSpec((1, LANES), const2),
                  pl.BlockSpec((1, LANES), const2),
                  pl.BlockSpec((1, HEAD_DIM), const2),
                  state],
        out_specs=[pl.BlockSpec((nb, L, A_VAL), lambda i, j: (i, j, 0)), state],
        out_shape=[jax.ShapeDtypeStruct((n, t, A_VAL), BF16),
                   jax.ShapeDtypeStruct((n, A_HEADS, HEAD_DIM, HEAD_DIM), F32)],
        scratch_shapes=[pltpu.VMEM((nb, L + SUBLANES, A_CONV_DIM), F32)],
        compiler_params=_cparams(("parallel", "arbitrary")),
    )(hbig, hbig, hgate, conv_buf8, conv_w, a_log, dt_bias, norm_g, s0)


def _gla_kernel(L, q_ref, k_ref, v_ref, r_ref, gate_ref, w2_ref, b2_ref, ng_ref, s0_ref,
                o_ref, s_ref):
    @pl.when(pl.program_id(1) == 0)
    def _():
        s_ref[0] = s0_ref[0]

    z = _dot(gate_ref[0].astype(BF16), w2_ref[...]) + b2_ref[...]
    lg = _log_sigmoid(z) * (1.0 / C_TAU)
    b = _dot_exact_lhs(_tri_incl(L), lg)
    causal = _iota2((L, L), 0) >= _iota2((L, L), 1)

    heads = []
    for h in range(C_HEADS):
        ks = slice(h * C_DK_PAD, (h + 1) * C_DK_PAD)
        vs = slice(h * C_DV, (h + 1) * C_DV)
        bh = b[:, ks]
        blast = bh[L - 1:L, :]
        bref = bh[L // 2:L // 2 + 1, :]
        q = q_ref[0, :, ks] * (C_DK ** -0.5)
        k = k_ref[0, :, ks]
        heads.append(dict(
            vs=vs, vb=v_ref[0, :, vs].astype(BF16), st=s_ref[0, h],
            q_state=(q * jnp.exp(bh)).astype(BF16),
            q_in=(q * jnp.exp(bh - bref)).astype(BF16),
            k_in=(k * jnp.exp(bref - bh)).astype(BF16),
            k_end=(k * jnp.exp(blast - bh)).astype(BF16),
            e_last=jnp.exp(blast)))
    for hd in heads:
        hd["att"] = jnp.where(causal, _dot_nt(hd["q_in"], hd["k_in"]), 0.0).astype(BF16)
        hd["o"] = _dot_nt(hd["q_state"], hd["st"].astype(BF16))
        hd["st_new"] = hd["st"] * hd["e_last"] + _dot_tn(hd["vb"], hd["k_end"])
    for hd in heads:
        o = hd["o"] + _dot(hd["att"], hd["vb"])
        hd["o"] = o * lax.rsqrt(jnp.mean(o * o, axis=-1, keepdims=True) + RMS_EPS) * ng_ref[...]
    for h, hd in enumerate(heads):
        s_ref[0, h] = hd["st_new"]
        o_ref[0, :, hd["vs"]] = (hd["o"] * _silu(r_ref[0, :, hd["vs"]])).astype(BF16)


def _gla(hbig, hgate, w2p, b2p, norm_g, s0t):
    n, t, _ = hbig.shape
    L = min(t, GLA_CHUNK)
    const2 = lambda i, j: (0, 0)
    state = pl.BlockSpec((1, C_HEADS, C_DV, C_DK_PAD), lambda i, j: (i, 0, 0, 0))
    return pl.pallas_call(
        functools.partial(_gla_kernel, L),
        grid=(n, t // L),
        in_specs=[pl.BlockSpec((1, L, C_KEY_PAD), lambda i, j: (i, j, COL_Q_C // C_KEY_PAD)),
                  pl.BlockSpec((1, L, C_KEY_PAD), lambda i, j: (i, j, COL_K_C // C_KEY_PAD)),
                  pl.BlockSpec((1, L, C_VAL), lambda i, j: (i, j, COL_V_C // C_VAL)),
                  pl.BlockSpec((1, L, C_VAL), lambda i, j: (i, j, COL_R_C // C_VAL)),
                  pl.BlockSpec((1, L, LANES), lambda i, j: (i, j, 0)),
                  pl.BlockSpec((LANES, C_KEY_PAD), const2),
                  pl.BlockSpec((1, C_KEY_PAD), const2),
                  pl.BlockSpec((1, C_DV), const2),
                  state],
        out_specs=[pl.BlockSpec((1, L, C_VAL), lambda i, j: (i, j, 0)), state],
        out_shape=[jax.ShapeDtypeStruct((n, t, C_VAL), BF16),
                   jax.ShapeDtypeStruct((n, C_HEADS, C_DV, C_DK_PAD), F32)],
        compiler_params=_cparams(("parallel", "arbitrary")),
    )(hbig, hbig, hbig, hbig, hgate, w2p, b2p, norm_g, s0t)


def _fox_prep_kernel(q_ref, k_ref, v_ref, c_ref, qp_ref, kp_ref, vp_ref):
    tb = q_ref.shape[1]
    lane = _iota2((tb, LANES), 1)
    ones_q = jnp.where((lane >= 3) & (lane < 6), 1.0, 0.0)
    ones_k = jnp.where(lane < 3, 1.0, 0.0)
    c2 = c_ref[0] * LOG2E
    for h in range(B_HEADS):
        hs = slice(h * HEAD_DIM, (h + 1) * HEAD_DIM)
        c1, cm, cl = (t.astype(F32) for t in _split3(c2[:, LANE_F_B + h:LANE_F_B + h + 1]))
        ext_q = jnp.where(lane == 0, c1, jnp.where(lane == 1, cm, jnp.where(lane == 2, cl, ones_q)))
        ext_k = jnp.where(lane == 3, -c1, jnp.where(lane == 4, -cm, jnp.where(lane == 5, -cl, ones_k)))
        base = 2 * h * HEAD_DIM
        qp_ref[0, :, base:base + HEAD_DIM] = (q_ref[0, :, hs] * (HEAD_DIM ** -0.5 * LOG2E)).astype(BF16)
        qp_ref[0, :, base + HEAD_DIM:base + 2 * HEAD_DIM] = ext_q.astype(BF16)
        kp_ref[0, :, base:base + HEAD_DIM] = k_ref[0, :, hs].astype(BF16)
        kp_ref[0, :, base + HEAD_DIM:base + 2 * HEAD_DIM] = ext_k.astype(BF16)
    vp_ref[0] = v_ref[0].astype(BF16)


def _fox_prep_cache_kernel(q_ref, k_ref, v_ref, c_ref, *rest):
    qp_ref, kp_ref, vp_ref, kc_ref, vc_ref = rest[-5:]
    _fox_prep_kernel(q_ref, k_ref, v_ref, c_ref, qp_ref, kp_ref, vp_ref)
    kc_ref[0] = k_ref[0]
    vc_ref[0] = v_ref[0]


def _fox_prep(hbig, c, layer, depth, caches, tb=FOX_PREP_TB):
    n, t, _ = hbig.shape
    tb = min(tb, t)
    col = lambda blk: (lambda i, j: (i, j, blk))
    wide = pl.BlockSpec((1, tb, 2 * B_WIDTH), col(0))
    cache_spec = pl.BlockSpec((None, 1, tb, B_WIDTH), lambda i, j: (layer, i, j, 0))
    cache_shape = jax.ShapeDtypeStruct((depth, n, t, B_WIDTH), F32)
    in_specs = [pl.BlockSpec((1, tb, B_WIDTH), col(COL_Q_B // B_WIDTH)),
                pl.BlockSpec((1, tb, B_WIDTH), col(COL_K_B // B_WIDTH)),
                pl.BlockSpec((1, tb, B_WIDTH), col(COL_V_B // B_WIDTH)),
                pl.BlockSpec((1, tb, LANES), col(0))]
    args = [hbig, hbig, hbig, c]
    aliases = {}
    if caches is not None:
        in_specs += [pl.BlockSpec(memory_space=pl.ANY)] * 2
        args += list(caches)
        aliases = {4: 3, 5: 4}
    return pl.pallas_call(
        _fox_prep_cache_kernel,
        grid=(n, t // tb),
        in_specs=in_specs,
        out_specs=[wide, wide, pl.BlockSpec((1, tb, B_WIDTH), col(0)), cache_spec, cache_spec],
        out_shape=[jax.ShapeDtypeStruct((n, t, 2 * B_WIDTH), BF16),
                   jax.ShapeDtypeStruct((n, t, 2 * B_WIDTH), BF16),
                   jax.ShapeDtypeStruct((n, t, B_WIDTH), BF16),
                   cache_shape, cache_shape],
        input_output_aliases=aliases,
        compiler_params=_cparams(("parallel", "parallel")),
    )(*args)


def _fox_prompt_kernel(tq, q_ref, k_ref, v_ref, o_ref, m_sc, l_sc, acc_sc):
    qi = pl.program_id(2)
    m_sc[...] = jnp.full(m_sc.shape, NEG_BIG, F32)
    l_sc[...] = jnp.zeros(l_sc.shape, F32)
    acc_sc[...] = jnp.zeros(acc_sc.shape, F32)
    n_tiles = tq // LANES
    heads = range(FOX_HEADS_PER_STEP)
    qw = 2 * HEAD_DIM

    def block(j, masked):
        off = pl.multiple_of(j * tq, tq)
        ss = [_dot_nt(q_ref[0, :, h * qw:(h + 1) * qw], k_ref[0, pl.ds(off, tq), h * qw:(h + 1) * qw])
              for h in heads]
        if masked:
            visible = _iota2((tq, tq), 1) <= _iota2((tq, tq), 0)
            ss = [jnp.where(visible, s, NEG_BIG) for s in ss]
        new = []
        for h, s in zip(heads, ss):
            tiles = [s[:, c * LANES:(c + 1) * LANES] for c in range(n_tiles)]
            mx = tiles[0]
            for tl in tiles[1:]:
                mx = jnp.maximum(mx, tl)
            m_prev = m_sc[h]
            m_new = jnp.maximum(m_prev, jnp.max(mx, axis=-1, keepdims=True))
            alpha = jnp.exp2(m_prev - m_new)
            ps = [jnp.exp2(tl - m_new) for tl in tiles]
            psum = ps[0]
            for pt in ps[1:]:
                psum = psum + pt
            p = jnp.concatenate([pt.astype(BF16) for pt in ps], axis=1)
            pv = _dot(p, v_ref[0, pl.ds(off, tq), h * HEAD_DIM:(h + 1) * HEAD_DIM])
            new.append((m_new, alpha * l_sc[h] + psum, alpha * acc_sc[h] + pv))
        for h, (m_new, l_new, acc_new) in zip(heads, new):
            m_sc[h] = m_new
            l_sc[h] = l_new
            acc_sc[h] = acc_new

    def body(j, carry):
        block(j, False)
        return carry

    lax.fori_loop(0, qi, body, 0)
    block(qi, True)
    for h in heads:
        o_ref[0, :, h * HEAD_DIM:(h + 1) * HEAD_DIM] = (
            acc_sc[h] / jnp.sum(l_sc[h], axis=-1, keepdims=True)).astype(BF16)


def _fox_prompt(qp, kp, vp, tq=FOX_TQ):
    n, t, _ = vp.shape
    tq = min(tq, t)
    hps = FOX_HEADS_PER_STEP
    return pl.pallas_call(
        functools.partial(_fox_prompt_kernel, tq),
        grid=(n, B_HEADS // hps, t // tq),
        in_specs=[pl.BlockSpec((1, tq, hps * 2 * HEAD_DIM), lambda i, h, qi: (i, qi, h)),
                  pl.BlockSpec((1, t, hps * 2 * HEAD_DIM), lambda i, h, qi: (i, 0, h),
                               pipeline_mode=pl.Buffered(1)),
                  pl.BlockSpec((1, t, hps * HEAD_DIM), lambda i, h, qi: (i, 0, h),
                               pipeline_mode=pl.Buffered(1))],
        out_specs=pl.BlockSpec((1, tq, hps * HEAD_DIM), lambda i, h, qi: (i, qi, h)),
        out_shape=jax.ShapeDtypeStruct((n, t, B_WIDTH), BF16),
        scratch_shapes=[pltpu.VMEM((hps, tq, LANES), F32),
                        pltpu.VMEM((hps, tq, LANES), F32),
                        pltpu.VMEM((hps, tq, HEAD_DIM), F32)],
        compiler_params=_cparams(("parallel", "parallel", "arbitrary")),
    )(qp, kp, vp)


def _fox_sample_kernel(t, p, q_ref, kn_ref, vn_ref, kp_ref, vp_ref, cq_ref, ckp_ref, ckn_ref, o_ref):
    causal = _iota2((t, t), 1) <= _iota2((t, t), 0)
    for h in range(B_HEADS):
        hs = slice(h * HEAD_DIM, (h + 1) * HEAD_DIM)
        kp = kp_ref[0, pl.ds(h, p, stride=B_HEADS), :].astype(BF16)
        vp = vp_ref[0, pl.ds(h, p, stride=B_HEADS), :].astype(BF16)
        qb = (q_ref[0, :, hs] * (HEAD_DIM ** -0.5)).astype(BF16)
        cq = cq_ref[0, h]
        sp = _dot_nt(qb, kp) + (cq - ckp_ref[0, h])
        sn = _dot_nt(qb, kn_ref[0, :, hs].astype(BF16)) + (cq - ckn_ref[0, h])
        sn = jnp.where(causal, sn, NEG_BIG)
        m = jnp.maximum(jnp.max(sp, axis=-1, keepdims=True), jnp.max(sn, axis=-1, keepdims=True))
        pp = jnp.exp(sp - m)
        pn = jnp.exp(sn - m)
        den = jnp.sum(pp, axis=-1, keepdims=True) + jnp.sum(pn, axis=-1, keepdims=True)
        o = _dot(pp.astype(BF16), vp) + _dot(pn.astype(BF16), vn_ref[0, :, hs].astype(BF16))
        o_ref[0, :, hs] = (o / den).astype(BF16)


def _fox_sample(hbig, k_past, v_past, layer, cq, ckp, ckn):
    n, t, _ = hbig.shape
    rows = k_past.shape[2]
    p = rows // B_HEADS
    col = lambda base: (lambda i: (i, 0, base // B_WIDTH))
    past = pl.BlockSpec((None, 1, rows, HEAD_DIM), lambda i: (layer, i, 0, 0))
    whole = lambda a: pl.BlockSpec((1,) + a.shape[1:], lambda i: (i, 0, 0, 0))
    return pl.pallas_call(
        functools.partial(_fox_sample_kernel, t, p),
        grid=(n,),
        in_specs=[pl.BlockSpec((1, t, B_WIDTH), col(COL_Q_B)),
                  pl.BlockSpec((1, t, B_WIDTH), col(COL_K_B)),
                  pl.BlockSpec((1, t, B_WIDTH), col(COL_V_B)),
                  past, past, whole(cq), whole(ckp), whole(ckn)],
        out_specs=pl.BlockSpec((1, t, B_WIDTH), lambda i: (i, 0, 0)),
        out_shape=jax.ShapeDtypeStruct((n, t, B_WIDTH), BF16),
        compiler_params=_cparams(("parallel",)),
    )(hbig, hbig, hbig, k_past, v_past, cq, ckp, ckn)


def _pad_cols(w, width):
    return jnp.pad(w, [(0, 0)] * (w.ndim - 1) + [(0, width - w.shape[-1])])


def _pad_heads_c(w):
    lead = w.shape[:-1]
    w = w.reshape(lead + (C_HEADS, C_DK))
    w = jnp.pad(w, [(0, 0)] * len(lead) + [(0, 0), (0, C_DK_PAD - C_DK)])
    return w.reshape(lead + (C_KEY_PAD,))


def _prep_w_in(w):
    o = 0
    parts = {}
    for name, size in (("qkv_a", A_CONV_DIM), ("z_a", A_VAL), ("a_a", A_HEADS), ("b_a", A_HEADS),
                       ("q_b", B_WIDTH), ("k_b", B_WIDTH), ("v_b", B_WIDTH), ("f_b", B_HEADS),
                       ("q_c", C_KEY), ("k_c", C_KEY), ("v_c", C_VAL), ("r_c", C_VAL), ("lr_c", C_RANK)):
        parts[name] = w[..., o:o + size].astype(BF16)
        o += size
    big = jnp.concatenate([parts["qkv_a"], parts["z_a"], parts["v_c"], parts["r_c"],
                           parts["q_b"], parts["k_b"], parts["v_b"],
                           _pad_heads_c(parts["q_c"]), _pad_heads_c(parts["k_c"])], axis=-1)
    gate = _pad_cols(jnp.concatenate([parts[nm] for nm in ("a_a", "b_a", "f_b", "lr_c")], axis=-1), LANES)
    return big, gate


def _pad_vec(v):
    return _pad_cols(v.reshape(1, -1).astype(F32), LANES)


def _mixer(x, xb, n, t, conv_buf8, s_a0, past_b, s_c0t, w, layer, depth, alpha, kv_caches):
    m = n * t
    hbig = _matmul(xb, w["w_big"], layer, PROJ_TM, PROJ_TN, F32).reshape(n, t, IN_BIG)
    hgate = _matmul(xb, w["w_gate4"], layer, PROJ_TM, IN_GATE, F32).reshape(n, t, IN_GATE)

    oa, s_a = _gdn(hbig, hgate, conv_buf8, w["conv_w"], w["a_log"], w["dt_bias"], w["a_norm_g"], s_a0)
    oc, s_ct = _gla(hbig, hgate, w["c_w2p"], w["c_b2p"], w["c_norm_g"], s_c0t)

    zero_c = jnp.zeros((n, 1, LANES), F32)
    if past_b is None:
        lf, c = _fgate(hgate, w["f_bias"], zero_c, True, FGATE_TB)
        qp, kp, vp, k_cache, v_cache = _fox_prep(hbig, c, layer, depth, kv_caches)
        kv_caches = (k_cache, v_cache)
        ob = _fox_prompt(qp, kp, vp)
        kb = vb = None
    else:
        k_past, v_past, lf_past = past_b
        p = lf_past.shape[1]
        lf_past = jnp.pad(lf_past.astype(F32), ((0, 0), (0, 0), (LANE_F_B, LANES - LANE_F_B - B_HEADS)))
        _, c_past = _fgate(lf_past, w["f_bias"], zero_c, False, FGATE_TB)
        lf, c = _fgate(hgate, w["f_bias"], c_past[:, p - 1:p, :], True, t)
        fb = slice(LANE_F_B, LANE_F_B + B_HEADS)
        cq = jnp.swapaxes(c[:, :, fb], 1, 2)[..., None]
        ckn = jnp.swapaxes(c[:, :, fb], 1, 2)[:, :, None, :]
        ckp = jnp.swapaxes(c_past[:, :, fb], 1, 2)[:, :, None, :]
        ob = _fox_sample(hbig, k_past, v_past, layer, cq, ckp, ckn)

    y = _outproj(oa.reshape(m, A_VAL), ob.reshape(m, B_WIDTH), oc.reshape(m, C_VAL),
                 w["w_out"], layer, x, alpha, PROJ_TM, PROJ_TN)

    conv_new = hbig[:, t - (CONV_W - 1):, COL_QKV_A:COL_QKV_A + A_CONV_DIM]
    if past_b is not None:
        kb = hbig[:, :, COL_K_B:COL_K_B + B_WIDTH].reshape(n, t, B_HEADS, HEAD_DIM)
        vb = hbig[:, :, COL_V_B:COL_V_B + B_WIDTH].reshape(n, t, B_HEADS, HEAD_DIM)
    s_c = jnp.swapaxes(s_ct, 2, 3)[:, :, :C_DK, :]
    return y, (conv_new, s_a, kb, vb, lf[:, :, LANE_F_B:LANE_F_B + B_HEADS], s_c), kv_caches


def _layer(x, xb, n, t, conv_buf8, s_a0, past_b, s_c0t, w, layer, depth, alpha, kv_caches=None):
    (v1, s1, s2), st, kv_caches = _mixer(x, xb, n, t, conv_buf8, s_a0, past_b, s_c0t, w, layer, depth,
                                         alpha, kv_caches)
    x1b, = _ln_apply(v1, s1, s2, w["ln1_g"], w["ln1_b"], (BF16,))
    hmid = _swiglu(x1b, w["w_gate"], w["w_up"], layer, FFN_TM, FFN_TN)
    v2, t1, t2 = _down_proj(hmid, w["w_down"], layer, v1, s1, s2, w["ln1_g"], w["ln1_b"], alpha,
                            DOWN_TM, DOWN_TN)
    if layer + 1 < depth:
        xb, x = _ln_apply(v2, t1, t2, w["ln2_g"], w["ln2_b"], (BF16, F32))
    else:
        xb, (x,) = None, _ln_apply(v2, t1, t2, w["ln2_g"], w["ln2_b"], (F32,))
    return x, xb, st, kv_caches


def kernel(x_prompt, x_sample, state_a_conv, state_a_rec, cache_b_k, cache_b_v, cache_b_logf, state_c_rec, w_in, conv_w, a_log, dt_bias, a_norm_g, f_bias, c_w2, c_b2, c_norm_g, w_out, ln1_g, ln1_b, w_gate, w_up, w_down, ln2_g, ln2_b):
    depth = w_in.shape[0]
    alpha = (2 * depth) ** DEPTH_ALPHA_POW
    nb, tp, _ = x_prompt.shape
    ns, ts, _ = x_sample.shape

    hp = x_prompt.reshape(nb * tp, D_MODEL).astype(F32)
    hs = x_sample.reshape(ns * ts, D_MODEL).astype(F32)
    hpb = hp.astype(BF16)
    hsb = hs.astype(BF16)
    p_states, s_states = [], []
    w_big, w_gate4 = _prep_w_in(w_in)
    w_out_b, w_down_b = w_out.astype(BF16), w_down.astype(BF16)
    past_len = cache_b_k.shape[2]
    cache_k = cache_b_k.reshape(depth, ns, past_len * B_HEADS, HEAD_DIM).astype(F32)
    cache_v = cache_b_v.reshape(depth, ns, past_len * B_HEADS, HEAD_DIM).astype(F32)
    prompt_kv = None
    for l in range(depth):
        w = {
            "w_big": w_big, "w_gate4": w_gate4,
            "conv_w": conv_w[l].astype(F32),
            "a_log": _pad_vec(a_log[l]), "dt_bias": _pad_vec(dt_bias[l]),
            "a_norm_g": a_norm_g[l].reshape(1, HEAD_DIM).astype(F32),
            "f_bias": jnp.pad(f_bias[l].reshape(1, B_HEADS).astype(F32),
                              ((0, 0), (LANE_F_B, LANES - LANE_F_B - B_HEADS))),
            "c_w2p": jnp.pad(_pad_heads_c(c_w2[l]),
                             ((LANE_LR_C, LANES - LANE_LR_C - C_RANK), (0, 0))).astype(BF16),
            "c_b2p": _pad_heads_c(c_b2[l].reshape(1, C_KEY)).astype(F32),
            "c_norm_g": c_norm_g[l].reshape(1, C_DV).astype(F32),
            "w_out": w_out_b,
            "ln1_g": ln1_g[l], "ln1_b": ln1_b[l], "ln2_g": ln2_g[l], "ln2_b": ln2_b[l],
            "w_gate": w_gate.astype(F32), "w_up": w_up.astype(F32), "w_down": w_down_b,
        }
        hp, hpb, stp, prompt_kv = _layer(
            hp, hpb, nb, tp,
            jnp.zeros((nb, SUBLANES, A_CONV_DIM), F32),
            jnp.zeros((nb, A_HEADS, HEAD_DIM, HEAD_DIM), F32),
            None,
            jnp.zeros((nb, C_HEADS, C_DV, C_DK_PAD), F32), w, l, depth, alpha, prompt_kv)
        p_states.append(stp)
        buf8 = jnp.pad(state_a_conv[l].astype(F32), ((0, 0), (SUBLANES - (CONV_W - 1), 0), (0, 0)))
        s_c0t = jnp.pad(jnp.swapaxes(state_c_rec[l].astype(F32), 2, 3),
                        ((0, 0), (0, 0), (0, 0), (0, C_DK_PAD - C_DK)))
        hs, hsb, sts, _ = _layer(
            hs, hsb, ns, ts, buf8, state_a_rec[l].astype(F32),
            (cache_k, cache_v, cache_b_logf[l]), s_c0t, w, l, depth, alpha)
        s_states.append(sts)

    dp, ds = x_prompt.dtype, x_sample.dtype
    stack = lambda states, i, dt: jnp.stack([s[i] for s in states], axis=0).astype(dt)
    prompt_k, prompt_v = (a.reshape(depth, nb, tp, B_HEADS, HEAD_DIM).astype(dp) for a in prompt_kv)
    return ((hp.reshape(nb, tp, D_MODEL).astype(dp), hs.reshape(ns, ts, D_MODEL).astype(ds))
            + (stack(p_states, 0, dp), stack(p_states, 1, dp), prompt_k, prompt_v,
               stack(p_states, 4, dp), stack(p_states, 5, dp))
            + tuple(stack(s_states, i, ds) for i in range(6)))
```
